```python
import math
import jax, jax.numpy as jnp
from jax import lax
import numpy as np

D_MODEL = 2048
BATCH = 4
SEQ = 2048
DEPTH = 4
DEC_BATCH = 8
DEC_SEQ = 4
PAST_LEN = 16384
PAGE_SIZE = 128

N_MIXERS = 2
N_A_LAYERS = (DEPTH + 1) // 2
N_B_LAYERS = DEPTH // 2
A_HEADS = 16
A_SUB_DIM = D_MODEL // (2 * A_HEADS)
A_HEAD_DIM = 2 * A_SUB_DIM
A_SCALE = A_SUB_DIM ** -0.5
B_GROUPS = ((128, 1), (512, 4), (2048, 16))
N_GROUPS = len(B_GROUPS)
B_HEADS_PER_GROUP = 8
B_HEAD_DIM = 128
B_HEADS = N_GROUPS * B_HEADS_PER_GROUP
B_SCALE = B_HEAD_DIM ** -0.5
D_FF = 5632
CONV_W = 3
Q_BLOCK = 128
EPS = 1e-6
NEG_INF = -1e30
POOL_NUM, POOL_DEN = 5, 4

kernel_name = "hybrid_diffattn_dilated_convffn_step"


def rms_norm(x, gain):
    xf = x.astype(jnp.float32)
    inv = lax.rsqrt(jnp.mean(xf * xf, axis=-1, keepdims=True) + EPS)
    return (xf * inv * gain.astype(jnp.float32)).astype(x.dtype)


def alibi_slopes(n):
    return 2.0 ** (-8.0 * jnp.arange(1, n + 1, dtype=jnp.float32) / n)


def diff_project(h, w_qkv, q_gain, k_gain):
    bsz, t, _ = h.shape
    qkv = (h @ w_qkv).reshape(bsz, t, 3, A_HEADS, A_HEAD_DIM)
    q = rms_norm(qkv[:, :, 0].reshape(bsz, t, A_HEADS, 2, A_SUB_DIM), q_gain)
    k = rms_norm(qkv[:, :, 1].reshape(bsz, t, A_HEADS, 2, A_SUB_DIM), k_gain)
    v = qkv[:, :, 2]
    return q, k, v


def diff_lambda(lam_vec, lam_init):
    lv = lam_vec.astype(jnp.float32)
    return jnp.exp(jnp.sum(lv[0] * lv[1])) - jnp.exp(jnp.sum(lv[2] * lv[3])) + lam_init


def diff_partial(q, k, v, q_pos, k_pos, slopes):
    dist = q_pos[:, None] - k_pos[None, :]
    bias = jnp.where(dist[None] >= 0,
                     -slopes[:, None, None] * dist[None].astype(jnp.float32), NEG_INF)
    s = jnp.einsum('bqhcd,bkhcd->bchqk', q, k).astype(jnp.float32) * A_SCALE + bias
    m = jnp.max(s, axis=-1, keepdims=True)
    e = jnp.exp(s - m)
    l = jnp.sum(e, axis=-1)
    acc = jnp.einsum('bchqk,bkhe->bchqe', e, v.astype(jnp.float32))
    return m[..., 0], l, acc


def diff_combine(m, l, acc, lam):
    m_all = jnp.max(m, axis=0)
    w = jnp.exp(m - m_all)
    denom = jnp.sum(w * l, axis=0)
    num = jnp.sum(w[..., None] * acc, axis=0)
    o = num / denom[..., None]
    return (o[:, 0] - lam * o[:, 1]).transpose(0, 2, 1, 3)


def diff_attn_prompt(q, k, v, slopes, lam):
    bsz, s_len = q.shape[:2]
    nb = s_len // Q_BLOCK
    q_blocks = q.reshape(bsz, nb, Q_BLOCK, A_HEADS, 2, A_SUB_DIM).swapaxes(0, 1)
    k_pos = jnp.arange(s_len, dtype=jnp.int32)

    def block(args):
        q_blk, start = args
        q_pos = start + jnp.arange(Q_BLOCK, dtype=jnp.int32)
        m, l, acc = diff_partial(q_blk, k, v, q_pos, k_pos, slopes)
        return diff_combine(m[None], l[None], acc[None], lam)

    o = lax.map(block, (q_blocks, jnp.arange(nb, dtype=jnp.int32) * Q_BLOCK))
    return o.swapaxes(0, 1).reshape(bsz, s_len, A_HEADS, A_HEAD_DIM)


def diff_attn_sample(q, k_new, v_new, cache_kv, layer, page_table, slopes, lam):
    dbsz, t = q.shape[:2]
    n_pages = page_table.shape[1]
    past = n_pages * PAGE_SIZE
    q_pos = past + jnp.arange(t, dtype=jnp.int32)

    def page(args):
        p, phys = args
        kv = cache_kv[layer, phys]
        kp = kv[:, :, 0].reshape(dbsz, PAGE_SIZE, A_HEADS, 2, A_SUB_DIM)
        k_pos = p * PAGE_SIZE + jnp.arange(PAGE_SIZE, dtype=jnp.int32)
        return diff_partial(q, kp, kv[:, :, 1], q_pos, k_pos, slopes)

    m, l, acc = lax.map(page, (jnp.arange(n_pages, dtype=jnp.int32), page_table.T))
    m_n, l_n, acc_n = diff_partial(q, k_new, v_new, q_pos, q_pos, slopes)
    m = jnp.concatenate([m, m_n[None]], axis=0)
    l = jnp.concatenate([l, l_n[None]], axis=0)
    acc = jnp.concatenate([acc, acc_n[None]], axis=0)
    return diff_combine(m, l, acc, lam)


def diff_output(o, sub_gain, lam_init, w_o):
    bsz, t = o.shape[:2]
    o = rms_norm(o, sub_gain) * (1.0 - lam_init)
    return o.reshape(bsz, t, A_HEADS * A_HEAD_DIM).astype(w_o.dtype) @ w_o


def dilated_project(h, w_qkv, q_gain, k_gain):
    bsz, t, _ = h.shape
    qkv = (h @ w_qkv).reshape(bsz, t, 3, N_GROUPS, B_HEADS_PER_GROUP, B_HEAD_DIM)
    q = rms_norm(qkv[:, :, 0], q_gain)
    k = rms_norm(qkv[:, :, 1], k_gain)
    return q, k, qkv[:, :, 2]


def dilated_prompt(q, k, v, window, dil, slopes):
    bsz, s_len, nh, hd = q.shape
    n_back = window // dil
    blk = n_back
    sub_len = s_len // dil
    nb = -(-sub_len // blk)
    sub_pad = nb * blk

    def to_sub(x):
        x = x.reshape(bsz, sub_len, dil, nh, hd).swapaxes(1, 2)
        return jnp.pad(x, ((0, 0), (0, 0), (0, sub_pad - sub_len), (0, 0), (0, 0)))

    def band(x):
        xp = jnp.pad(x, ((0, 0), (0, 0), (blk, 0), (0, 0), (0, 0)))
        prev = xp[:, :, :sub_pad].reshape(bsz, dil, nb, blk, nh, hd)
        cur = x.reshape(bsz, dil, nb, blk, nh, hd)
        return jnp.concatenate([prev, cur], axis=3)

    qb = to_sub(q).reshape(bsz, dil, nb, blk, nh, hd)
    kb = band(to_sub(k))
    vb = band(to_sub(v))
    qi = jnp.arange(blk, dtype=jnp.int32)[:, None]
    ci = jnp.arange(2 * blk, dtype=jnp.int32)[None, :]
    sub_dist = qi + blk - ci
    key_idx = jnp.arange(nb, dtype=jnp.int32)[:, None, None] * blk - blk + ci[None]
    valid = (sub_dist >= 0)[None] & (sub_dist <= n_back)[None] & (key_idx >= 0)
    bias = -slopes[:, None, None] * (sub_dist * dil).astype(jnp.float32)[None]
    s = jnp.einsum('brnqhd,brnkhd->brnhqk', qb, kb).astype(jnp.float32) * B_SCALE
    s = jnp.where(valid[None, None, :, None], s + bias[None, None, None], NEG_INF)
    lse = jax.nn.logsumexp(s, axis=-1)
    p = jnp.exp(s - lse[..., None])
    o = jnp.einsum('brnhqk,brnkhd->brnqhd', p, vb.astype(jnp.float32))
    o = o.reshape(bsz, dil, sub_pad, nh, hd)[:, :, :sub_len].swapaxes(1, 2).reshape(bsz, s_len, nh, hd)
    lse = lse.transpose(0, 1, 2, 4, 3).reshape(bsz, dil, sub_pad, nh)[:, :, :sub_len]
    lse = lse.swapaxes(1, 2).reshape(bsz, s_len, nh)
    return o, lse


def dilated_sample(q, k_all, v_all, window, dil, slopes):
    t = q.shape[1]
    wb = k_all.shape[1] - t
    n_back = window // dil
    steps = jnp.arange(n_back + 1, dtype=jnp.int32)
    idx = wb + jnp.arange(t, dtype=jnp.int32)[:, None] - dil * steps[None, :]
    valid = idx >= 0
    idx = jnp.maximum(idx, 0)
    kg = k_all[:, idx]
    vg = v_all[:, idx]
    bias = -slopes[:, None, None] * (dil * steps).astype(jnp.float32)[None, None, :]
    s = jnp.einsum('bqhd,bqnhd->bhqn', q, kg).astype(jnp.float32) * B_SCALE
    s = jnp.where(valid[None, None], s + bias, NEG_INF)
    lse = jax.nn.logsumexp(s, axis=-1)
    p = jnp.exp(s - lse[..., None])
    o = jnp.einsum('bhqn,bqnhd->bqhd', p, vg.astype(jnp.float32))
    return o, lse.transpose(0, 2, 1)


def merge_groups(outs, lses):
    o = jnp.stack(outs, axis=0)
    alpha = jax.nn.softmax(jnp.stack(lses, axis=0), axis=0)
    return jnp.sum(alpha[..., None] * o, axis=0)


def dilated_output(o, w_o):
    bsz, t = o.shape[:2]
    return o.reshape(bsz, t, B_HEADS_PER_GROUP * B_HEAD_DIM).astype(w_o.dtype) @ w_o


def conv_ffn(h, conv_buf, w_gate, w_up, w_down, conv_w, conv_b):
    t = h.shape[1]
    g = h @ w_gate
    u = h @ w_up
    gp = jnp.concatenate([conv_buf.astype(g.dtype), g], axis=1)
    c = conv_b
    for j in range(CONV_W):
        c = c + gp[:, j:j + t] * conv_w[j]
    y = (jax.nn.silu(c) * u) @ w_down
    return y, gp[:, -(CONV_W - 1):]


def setup_inputs(seed: int = 0) -> dict:
    key = jax.random.key(seed)
    ks = jax.random.split(key, 32)
    f32 = jnp.float32
    n_pages = PAST_LEN // PAGE_SIZE
    n_pool = (DEC_BATCH * n_pages * POOL_NUM) // POOL_DEN

    def nrm(k, shape, scale=1.0):
        return scale * jax.random.normal(k, shape, f32)

    def gain(k, shape):
        return 1.0 + 0.02 * jax.random.normal(k, shape, f32)

    inter = B_HEADS_PER_GROUP * B_HEAD_DIM
    return {
        "x_prompt": nrm(ks[0], (BATCH, SEQ, D_MODEL)),
        "x_sample": nrm(ks[1], (DEC_BATCH, DEC_SEQ, D_MODEL)),
        "cache_kv_diff": nrm(ks[2], (N_A_LAYERS, n_pool, PAGE_SIZE, 2, A_HEADS, A_HEAD_DIM)),
        "state_kv_w128": nrm(ks[3], (N_B_LAYERS, DEC_BATCH, min(B_GROUPS[0][0], PAST_LEN), 2, B_HEADS_PER_GROUP, B_HEAD_DIM)),
        "state_kv_w512": nrm(ks[4], (N_B_LAYERS, DEC_BATCH, min(B_GROUPS[1][0], PAST_LEN), 2, B_HEADS_PER_GROUP, B_HEAD_DIM)),
        "state_kv_w2048": nrm(ks[5], (N_B_LAYERS, DEC_BATCH, min(B_GROUPS[2][0], PAST_LEN), 2, B_HEADS_PER_GROUP, B_HEAD_DIM)),
        "state_conv": nrm(ks[6], (DEPTH, DEC_BATCH, CONV_W - 1, D_FF)),
        "page_table": jax.random.permutation(ks[7], n_pool)[:DEC_BATCH * n_pages].reshape(DEC_BATCH, n_pages).astype(jnp.int32),
        "attn_norm": gain(ks[8], (DEPTH, D_MODEL)),
        "ffn_norm": gain(ks[9], (DEPTH, D_MODEL)),
        "a_w_qkv": nrm(ks[10], (N_A_LAYERS, D_MODEL, 3 * A_HEADS * A_HEAD_DIM), D_MODEL ** -0.5),
        "a_w_o": nrm(ks[11], (N_A_LAYERS, A_HEADS * A_HEAD_DIM, D_MODEL), (A_HEADS * A_HEAD_DIM) ** -0.5),
        "a_q_gain": gain(ks[12], (N_A_LAYERS, A_SUB_DIM)),
        "a_k_gain": gain(ks[13], (N_A_LAYERS, A_SUB_DIM)),
        "a_lambda": nrm(ks[14], (N_A_LAYERS, 4, A_SUB_DIM), 0.1),
        "a_sub_gain": gain(ks[15], (N_A_LAYERS, A_HEAD_DIM)),
        "b_w_qkv": nrm(ks[16], (N_B_LAYERS, D_MODEL, 3 * B_HEADS * B_HEAD_DIM), D_MODEL ** -0.5),
        "b_w_o": nrm(ks[17], (N_B_LAYERS, inter, D_MODEL), inter ** -0.5),
        "b_q_gain": gain(ks[18], (N_B_LAYERS, B_HEAD_DIM)),
        "b_k_gain": gain(ks[19], (N_B_LAYERS, B_HEAD_DIM)),
        "ffn_w_gate": nrm(ks[20], (DEPTH, D_MODEL, D_FF), D_MODEL ** -0.5),
        "ffn_w_up": nrm(ks[21], (DEPTH, D_MODEL, D_FF), D_MODEL ** -0.5),
        "ffn_w_down": nrm(ks[22], (DEPTH, D_FF, D_MODEL), D_FF ** -0.5),
        "ffn_conv_w": nrm(ks[23], (DEPTH, CONV_W, D_FF), CONV_W ** -0.5),
        "ffn_conv_b": nrm(ks[24], (DEPTH, D_FF), 0.02),
    }


def reference(x_prompt, x_sample, cache_kv_diff, state_kv_w128, state_kv_w512, state_kv_w2048,
              state_conv, page_table, attn_norm, ffn_norm, a_w_qkv, a_w_o, a_q_gain, a_k_gain,
              a_lambda, a_sub_gain, b_w_qkv, b_w_o, b_q_gain, b_k_gain, ffn_w_gate, ffn_w_up,
              ffn_w_down, ffn_conv_w, ffn_conv_b):
    xp, xs = x_prompt, x_sample
    bsz, s_len = xp.shape[:2]
    t = xs.shape[1]
    win_states = (state_kv_w128, state_kv_w512, state_kv_w2048)
    a_slopes = alibi_slopes(A_HEADS)
    b_slopes = alibi_slopes(B_HEADS).reshape(N_GROUPS, B_HEADS_PER_GROUP)
    kv_diff_p, kv_diff_s = [], []
    win_p = [[] for _ in B_GROUPS]
    win_s = [[] for _ in B_GROUPS]
    conv_p, conv_s = [], []

    for i in range(DEPTH):
        hp = rms_norm(xp, attn_norm[i])
        hs = rms_norm(xs, attn_norm[i])
        if i % N_MIXERS == 0:
            a = i // N_MIXERS
            lam_init = 0.8 - 0.6 * math.exp(-0.3 * i)
            lam = diff_lambda(a_lambda[a], lam_init)
            qp, kp, vp = diff_project(hp, a_w_qkv[a], a_q_gain[a], a_k_gain[a])
            qs, ks_, vs = diff_project(hs, a_w_qkv[a], a_q_gain[a], a_k_gain[a])
            op = diff_attn_prompt(qp, kp, vp, a_slopes, lam)
            os_ = diff_attn_sample(qs, ks_, vs, cache_kv_diff, a, page_table, a_slopes, lam)
            xp = xp + diff_output(op, a_sub_gain[a], lam_init, a_w_o[a])
            xs = xs + diff_output(os_, a_sub_gain[a], lam_init, a_w_o[a])
            kv_diff_p.append(jnp.stack([kp.reshape(bsz, s_len, A_HEADS, A_HEAD_DIM), vp], axis=2))
            kv_diff_s.append(jnp.stack([ks_.reshape(xs.shape[0], t, A_HEADS, A_HEAD_DIM), vs], axis=2))
        else:
            b = i // N_MIXERS
            qp, kp, vp = dilated_project(hp, b_w_qkv[b], b_q_gain[b], b_k_gain[b])
            qs, ks_, vs = dilated_project(hs, b_w_qkv[b], b_q_gain[b], b_k_gain[b])
            outs_p, lses_p, outs_s, lses_s = [], [], [], []
            for g, (window, dil) in enumerate(B_GROUPS):
                o, lse = dilated_prompt(qp[:, :, g], kp[:, :, g], vp[:, :, g], window, dil, b_slopes[g])
                outs_p.append(o)
                lses_p.append(lse)
                full = jnp.concatenate([win_states[g][b].astype(ks_.dtype),
                                        jnp.stack([ks_[:, :, g], vs[:, :, g]], axis=2)], axis=1)
                o, lse = dilated_sample(qs[:, :, g], full[:, :, 0], full[:, :, 1], window, dil, b_slopes[g])
                outs_s.append(o)
                lses_s.append(lse)
                keep = min(window, s_len)
                win_p[g].append(jnp.stack([kp[:, :, g], vp[:, :, g]], axis=2)[:, s_len - keep:])
                win_s[g].append(full[:, t:])
            xp = xp + dilated_output(merge_groups(outs_p, lses_p), b_w_o[b])
            xs = xs + dilated_output(merge_groups(outs_s, lses_s), b_w_o[b])
        hp = rms_norm(xp, ffn_norm[i])
        hs = rms_norm(xs, ffn_norm[i])
        yp, cbp = conv_ffn(hp, jnp.zeros((bsz, CONV_W - 1, D_FF), hp.dtype), ffn_w_gate[i],
                           ffn_w_up[i], ffn_w_down[i], ffn_conv_w[i], ffn_conv_b[i])
        ys, cbs = conv_ffn(hs, state_conv[i], ffn_w_gate[i], ffn_w_up[i], ffn_w_down[i],
                           ffn_conv_w[i], ffn_conv_b[i])
        xp = xp + yp
        xs = xs + ys
        conv_p.append(cbp)
        conv_s.append(cbs)

    return (xp, xs, jnp.stack(kv_diff_p), jnp.stack(kv_diff_s),
            jnp.stack(win_p[0]), jnp.stack(win_s[0]), jnp.stack(win_p[1]), jnp.stack(win_s[1]),
            jnp.stack(win_p[2]), jnp.stack(win_s[2]), jnp.stack(conv_p), jnp.stack(conv_s))
```

```python
import functools
import math

import jax
import jax.numpy as jnp
from jax import lax
from jax.experimental import pallas as pl
from jax.experimental.pallas import tpu as pltpu

EPS = 1e-6
NEG_INF = -1e30
LANES = 128
B_GROUPS = ((128, 1), (512, 4), (2048, 16))
VMEM_LIMIT = 56 * 1024 * 1024

_BF16 = jnp.bfloat16
_F32 = jnp.float32
_NT = (((1,), (1,)), ((), ()))


def _params(*sem):
    return pltpu.CompilerParams(dimension_semantics=sem, vmem_limit_bytes=VMEM_LIMIT)


def _alibi_slopes(n):
    return 2.0 ** (-8.0 * jnp.arange(1, n + 1, dtype=_F32) / n)


def _rmsnorm_kernel(x_ref, g_ref, o_ref):
    x = x_ref[...]
    inv = lax.rsqrt(jnp.mean(x * x, axis=-1, keepdims=True) + EPS)
    o_ref[...] = (x * inv * g_ref[...]).astype(o_ref.dtype)


def rmsnorm_bf16(x, gain, tm):
    m, d = x.shape
    return pl.pallas_call(
        _rmsnorm_kernel,
        grid=(pl.cdiv(m, tm),),
        in_specs=[pl.BlockSpec((tm, d), lambda i: (i, 0)),
                  pl.BlockSpec((1, d), lambda i: (0, 0))],
        out_specs=pl.BlockSpec((tm, d), lambda i: (i, 0)),
        out_shape=jax.ShapeDtypeStruct((m, d), _BF16),
        compiler_params=_params("parallel"),
        name="rmsnorm",
    )(x, gain.reshape(1, d))


def _cast_weight(w_ref, wb_ref):
    @pl.when(pl.program_id(1) == 0)
    def _():
        wb_ref[...] = w_ref[...].astype(_BF16)


def _mm_res_kernel(a_ref, w_ref, r_ref, o_ref, wb_ref):
    _cast_weight(w_ref, wb_ref)
    acc = jnp.dot(a_ref[...], wb_ref[...], preferred_element_type=_F32)
    o_ref[...] = r_ref[...] + acc


def matmul_residual(a, w, res, tm, tn):
    m, k = a.shape
    n = w.shape[1]
    return pl.pallas_call(
        _mm_res_kernel,
        grid=(n // tn, pl.cdiv(m, tm)),
        in_specs=[pl.BlockSpec((tm, k), lambda j, i: (i, 0)),
                  pl.BlockSpec((k, tn), lambda j, i: (0, j)),
                  pl.BlockSpec((tm, tn), lambda j, i: (i, j))],
        out_specs=pl.BlockSpec((tm, tn), lambda j, i: (i, j)),
        out_shape=jax.ShapeDtypeStruct((m, n), _F32),
        scratch_shapes=[pltpu.VMEM((k, tn), _BF16)],
        compiler_params=_params("arbitrary", "arbitrary"),
        name="matmul_residual",
    )(a, w, res)


def _head_rms(c, gain, group):
    sq = c * c
    if group == LANES:
        ms = jnp.sum(sq, axis=-1, keepdims=True) * (1.0 / group)
    else:
        lane = lax.broadcasted_iota(jnp.int32, c.shape, 1)
        low = lane < group
        s_lo = jnp.sum(jnp.where(low, sq, 0.0), axis=-1, keepdims=True)
        s_hi = jnp.sum(jnp.where(low, 0.0, sq), axis=-1, keepdims=True)
        ms = jnp.where(low, s_lo, s_hi) * (1.0 / group)
    return c * lax.rsqrt(ms + EPS) * gain


def _mm_norm_kernel(a_ref, w_ref, g_ref, o_ref, wb_ref, *, group, norm_blocks, n_blocks):
    _cast_weight(w_ref, wb_ref)
    acc = jnp.dot(a_ref[...], wb_ref[...], preferred_element_type=_F32)
    tn = acc.shape[1]

    def normed():
        g = g_ref[...]
        for c in range(tn // LANES):
            sl = slice(c * LANES, (c + 1) * LANES)
            o_ref[:, sl] = _head_rms(acc[:, sl], g, group).astype(o_ref.dtype)

    def raw():
        o_ref[...] = acc.astype(o_ref.dtype)

    if norm_blocks >= n_blocks:
        normed()
    elif norm_blocks == 0:
        raw()
    else:
        pl.when(pl.program_id(0) < norm_blocks)(normed)
        pl.when(pl.program_id(0) >= norm_blocks)(raw)


def matmul_headnorm(a, w, gain, col0, ncols, norm_cols, group, out_dtype, tm, tn):
    m, k = a.shape
    assert col0 % tn == 0 and ncols % tn == 0 and norm_cols % tn == 0
    assert LANES % group == 0 and tn % LANES == 0
    off = col0 // tn
    n_blocks = ncols // tn
    g = jnp.tile(gain.astype(_F32), LANES // group).reshape(1, LANES)
    return pl.pallas_call(
        functools.partial(_mm_norm_kernel, group=group, norm_blocks=norm_cols // tn,
                          n_blocks=n_blocks),
        grid=(n_blocks, pl.cdiv(m, tm)),
        in_specs=[pl.BlockSpec((tm, k), lambda j, i: (i, 0)),
                  pl.BlockSpec((k, tn), lambda j, i: (0, j + off)),
                  pl.BlockSpec((1, LANES), lambda j, i: (0, 0))],
        out_specs=pl.BlockSpec((tm, tn), lambda j, i: (i, j)),
        out_shape=jax.ShapeDtypeStruct((m, ncols), out_dtype),
        scratch_shapes=[pltpu.VMEM((k, tn), _BF16)],
        compiler_params=_params("arbitrary", "arbitrary"),
        name="matmul_headnorm",
    )(a, w, g)


def _gated(g, g1, g2, u, cw_ref, cb_ref):
    cw = cw_ref[...]
    c = cb_ref[...] + g2 * cw[0:1] + g1 * cw[1:2] + g * cw[2:3]
    return (c * jax.nn.sigmoid(c) * u).astype(_BF16)


def _ffn_prompt_kernel(a_ref, wg_ref, wu_ref, cw_ref, cb_ref, act_ref, tail_ref,
                       wgb_ref, wub_ref, carry_ref, *, tiles_per_seq):
    _cast_weight(wg_ref, wgb_ref)
    _cast_weight(wu_ref, wub_ref)
    a = a_ref[...]
    g = jnp.dot(a, wgb_ref[...], preferred_element_type=_F32)
    u = jnp.dot(a, wub_ref[...], preferred_element_type=_F32)
    tm = g.shape[0]

    @pl.when(pl.program_id(1) % tiles_per_seq == 0)
    def _():
        carry_ref[...] = jnp.zeros_like(carry_ref)

    prev = carry_ref[...]
    row = lax.broadcasted_iota(jnp.int32, g.shape, 0)
    g1 = jnp.where(row == 0, prev[7:8], pltpu.roll(g, 1, axis=0))
    g2 = jnp.where(row == 0, prev[6:7], jnp.where(row == 1, prev[7:8], pltpu.roll(g, 2, axis=0)))
    act_ref[...] = _gated(g, g1, g2, u, cw_ref, cb_ref)
    tail = g[tm - 8:tm]
    carry_ref[...] = tail
    tail_ref[...] = tail


def ffn_gate_up_prompt(a, w_gate, w_up, conv_w, conv_b, seq_len, tm, tn):
    m, k = a.shape
    f = w_gate.shape[1]
    assert seq_len % tm == 0 and m % seq_len == 0 and f % tn == 0
    tps = seq_len // tm
    return pl.pallas_call(
        functools.partial(_ffn_prompt_kernel, tiles_per_seq=tps),
        grid=(f // tn, m // tm),
        in_specs=[pl.BlockSpec((tm, k), lambda j, i: (i, 0)),
                  pl.BlockSpec((k, tn), lambda j, i: (0, j)),
                  pl.BlockSpec((k, tn), lambda j, i: (0, j)),
                  pl.BlockSpec((3, tn), lambda j, i: (0, j)),
                  pl.BlockSpec((1, tn), lambda j, i: (0, j))],
        out_specs=[pl.BlockSpec((tm, tn), lambda j, i: (i, j)),
                   pl.BlockSpec((None, 8, tn), lambda j, i: (i // tps, 0, j))],
        out_shape=[jax.ShapeDtypeStruct((m, f), _BF16),
                   jax.ShapeDtypeStruct((m // seq_len, 8, f), _F32)],
        scratch_shapes=[pltpu.VMEM((k, tn), _BF16), pltpu.VMEM((k, tn), _BF16),
                        pltpu.VMEM((8, tn), _F32)],
        compiler_params=_params("arbitrary", "arbitrary"),
        name="ffn_gate_up_prompt",
    )(a, w_gate, w_up, conv_w, conv_b.reshape(1, f))


def _ffn_sample_kernel(a_ref, wg_ref, wu_ref, cw_ref, cb_ref, s0_ref, s1_ref, act_ref, g_ref,
                       *, steps):
    a = a_ref[...]
    g = jnp.dot(a, wg_ref[...].astype(_BF16), preferred_element_type=_F32)
    u = jnp.dot(a, wu_ref[...].astype(_BF16), preferred_element_type=_F32)
    pos = lax.broadcasted_iota(jnp.int32, g.shape, 0) % steps
    s0 = s0_ref[...]
    s1 = s1_ref[...]
    g1 = jnp.where(pos == 0, s1, pltpu.roll(g, 1, axis=0))
    g2 = jnp.where(pos == 0, s0, jnp.where(pos == 1, s1, pltpu.roll(g, 2, axis=0)))
    act_ref[...] = _gated(g, g1, g2, u, cw_ref, cb_ref)
    g_ref[...] = g


def ffn_gate_up_sample(a, w_gate, w_up, conv_w, conv_b, state, tn):
    m, k = a.shape
    f = w_gate.shape[1]
    db = state.shape[0]
    steps = m // db
    assert steps >= 2 and f % tn == 0
    s0 = jnp.repeat(state[:, 0], steps, axis=0)
    s1 = jnp.repeat(state[:, 1], steps, axis=0)
    col = lambda j: (0, j)
    return pl.pallas_call(
        functools.partial(_ffn_sample_kernel, steps=steps),
        grid=(f // tn,),
        in_specs=[pl.BlockSpec((m, k), lambda j: (0, 0)),
                  pl.BlockSpec((k, tn), col), pl.BlockSpec((k, tn), col),
                  pl.BlockSpec((3, tn), col), pl.BlockSpec((1, tn), col),
                  pl.BlockSpec((m, tn), col), pl.BlockSpec((m, tn), col)],
        out_specs=[pl.BlockSpec((m, tn), col), pl.BlockSpec((m, tn), col)],
        out_shape=[jax.ShapeDtypeStruct((m, f), _BF16), jax.ShapeDtypeStruct((m, f), _F32)],
        compiler_params=_params("arbitrary"),
        name="ffn_gate_up_sample",
    )(a, w_gate, w_up, conv_w, conv_b.reshape(1, f), s0, s1)


def _diff_lambda(lam_ref, lam_init):
    lv = lam_ref[...]
    a = jnp.sum(lv[0:1] * lv[1:2], axis=-1, keepdims=True)
    b = jnp.sum(lv[2:3] * lv[3:4], axis=-1, keepdims=True)
    return jnp.exp(a) - jnp.exp(b) + lam_init


def _sub_norm(o, sg_ref, lam_init):
    inv = lax.rsqrt(jnp.mean(o * o, axis=-1, keepdims=True) + EPS)
    return o * inv * sg_ref[...] * (1.0 - lam_init)


def _split_sub_heads(q, rows_first_map):
    lane = lax.broadcasted_iota(jnp.int32, q.shape, 1) % LANES
    row = lax.broadcasted_iota(jnp.int32, q.shape, 0)
    keep = (lane < LANES // 2) == (row < rows_first_map)
    return jnp.where(keep, q, jnp.zeros_like(q))


def _diff_prompt_kernel(slopes_ref, lam_ref, q_ref, k_ref, v_ref, sg_ref, o_ref,
                        kb_ref, vb_ref, *, tq, lam_init, scale):
    h = pl.program_id(1)
    qi = pl.program_id(2)

    @pl.when(qi == 0)
    def _():
        kb_ref[...] = k_ref[...].astype(_BF16)
        vb_ref[...] = v_ref[...].astype(_BF16)

    slope = slopes_ref[h]
    q = q_ref[...]
    q2 = _split_sub_heads(jnp.concatenate([q, q], axis=0), tq)
    q0 = qi * tq
    rel = (lax.broadcasted_iota(jnp.int32, (2 * tq, tq), 0) % tq
           - lax.broadcasted_iota(jnp.int32, (2 * tq, tq), 1))

    def body(j, carry):
        m, l, acc = carry
        k0 = pl.multiple_of(j * tq, tq)
        kb = kb_ref[pl.ds(k0, tq), :]
        vb = vb_ref[pl.ds(k0, tq), :]
        s = lax.dot_general(q2, kb, _NT, preferred_element_type=_F32) * scale
        dist = rel + (q0 - k0)
        s = jnp.where(dist >= 0, s - slope * dist.astype(_F32), NEG_INF)
        m_new = jnp.maximum(m, jnp.max(s, axis=-1, keepdims=True))
        alpha = jnp.exp(m - m_new)
        p = jnp.exp(s - m_new)
        l = alpha * l + jnp.sum(p, axis=-1, keepdims=True)
        acc = alpha * acc + jnp.dot(p.astype(_BF16), vb, preferred_element_type=_F32)
        return m_new, l, acc

    init = (jnp.full((2 * tq, 1), NEG_INF, _F32), jnp.zeros((2 * tq, 1), _F32),
            jnp.zeros((2 * tq, LANES), _F32))
    _, l, acc = lax.fori_loop(0, qi + 1, body, init)
    o = acc / l
    lam = _diff_lambda(lam_ref, lam_init)
    od = o[:tq] - lam * o[tq:]
    o_ref[...] = _sub_norm(od, sg_ref, lam_init).astype(o_ref.dtype)


def diff_attn_prompt(q, kv, lam_vec, sub_gain, lam_init, bsz, seq_len, n_heads, tq):
    nq = seq_len // tq
    slopes = _alibi_slopes(n_heads)
    return pl.pallas_call(
        functools.partial(_diff_prompt_kernel, tq=tq, lam_init=lam_init,
                          scale=(LANES // 2) ** -0.5),
        grid=(bsz, n_heads, nq),
        in_specs=[pl.BlockSpec(memory_space=pltpu.SMEM),
                  pl.BlockSpec(lam_vec.shape, lambda b, h, i: (0, 0)),
                  pl.BlockSpec((tq, LANES), lambda b, h, i: (b * nq + i, h)),
                  pl.BlockSpec((seq_len, LANES), lambda b, h, i: (b, h)),
                  pl.BlockSpec((seq_len, LANES), lambda b, h, i: (b, n_heads + h)),
                  pl.BlockSpec((1, LANES), lambda b, h, i: (0, 0))],
        out_specs=pl.BlockSpec((tq, LANES), lambda b, h, i: (b * nq + i, h)),
        out_shape=jax.ShapeDtypeStruct((bsz * seq_len, n_heads * LANES), _BF16),
        scratch_shapes=[pltpu.VMEM((seq_len, LANES), _BF16), pltpu.VMEM((seq_len, LANES), _BF16)],
        compiler_params=_params("arbitrary", "arbitrary", "arbitrary"),
        name="diff_attn_prompt",
    )(slopes, lam_vec, q, kv, kv, sub_gain.reshape(1, LANES))


def _diff_sample_kernel(pt_ref, q_ref, page_ref, new_ref, slope_ref, lam_ref, sg_ref, o_ref,
                        q2_ref, m_ref, l_ref, acc_ref,
                        *, n_heads, steps, page, n_pages, lam_init, scale):
    del pt_ref
    p = pl.program_id(1)
    hw = n_heads * LANES
    rows = 8 * n_heads
    assert steps == 4

    @pl.when(p == 0)
    def _():
        q2_ref[...] = _split_sub_heads(q_ref[...], steps)
        m_ref[...] = jnp.full_like(m_ref, NEG_INF)
        l_ref[...] = jnp.zeros_like(l_ref)
        acc_ref[...] = jnp.zeros_like(acc_ref)

    slope = slope_ref[...]
    step = (lax.broadcasted_iota(jnp.int32, (rows, 1), 0) % steps)

    def attend(kv_ref, nk, bias):
        s = jnp.concatenate([
            lax.dot_general(q2_ref[:, h * LANES:(h + 1) * LANES],
                            kv_ref[:, h * LANES:(h + 1) * LANES].astype(_BF16), _NT,
                            preferred_element_type=_F32)
            for h in range(n_heads)], axis=0)
        s = s * scale + bias
        m = m_ref[...]
        m_new = jnp.maximum(m, jnp.max(s, axis=-1, keepdims=True))
        alpha = jnp.exp(m - m_new)
        e = jnp.exp(s - m_new)
        l_ref[...] = alpha * l_ref[...] + jnp.sum(e, axis=-1, keepdims=True)
        m_ref[...] = m_new
        pv = jnp.concatenate([
            jnp.dot(e[h * 8:(h + 1) * 8].astype(_BF16),
                    kv_ref[:, hw + h * LANES:hw + (h + 1) * LANES].astype(_BF16),
                    preferred_element_type=_F32)
            for h in range(n_heads)], axis=0)
        acc_ref[...] = alpha * acc_ref[...] + pv

    past = n_pages * page
    key = lax.broadcasted_iota(jnp.int32, (rows, page), 1)
    dist = (past - p * page) + step - key
    attend(page_ref, page, -slope * dist.astype(_F32))

    @pl.when(p == n_pages - 1)
    def _():
        nk = new_ref.shape[0]
        key_n = lax.broadcasted_iota(jnp.int32, (rows, nk), 1)
        d_n = step - key_n
        attend(new_ref, nk, jnp.where(d_n >= 0, -slope * d_n.astype(_F32), NEG_INF))
        o = acc_ref[...] / l_ref[...]
        lam = _diff_lambda(lam_ref, lam_init)
        od = o - lam * pltpu.roll(o, rows - steps, axis=0)
        o_ref[...] = _sub_norm(od, sg_ref, lam_init)


def diff_attn_sample(q, kv_new, cache, layer, page_table, lam_vec, sub_gain, lam_init,
                     n_heads, steps):
    db, n_pages = page_table.shape
    page = cache.shape[2]
    hw = n_heads * LANES
    rows = 8 * n_heads
    q8 = jnp.tile(q.reshape(db, steps, hw), (1, 2, 1))
    new16 = jnp.pad(kv_new.reshape(db, steps, 2 * hw), ((0, 0), (0, 16 - steps), (0, 0)))
    slope_rows = jnp.repeat(_alibi_slopes(n_heads), 8).reshape(rows, 1)
    grid_spec = pltpu.PrefetchScalarGridSpec(
        num_scalar_prefetch=1,
        grid=(db, n_pages),
        in_specs=[pl.BlockSpec((None, 8, hw), lambda b, p, pt: (b, 0, 0)),
                  pl.BlockSpec((None, None, page, 2 * hw),
                               lambda b, p, pt: (layer, pt[b * n_pages + p], 0, 0)),
                  pl.BlockSpec((None, 16, 2 * hw), lambda b, p, pt: (b, 0, 0)),
                  pl.BlockSpec((rows, 1), lambda b, p, pt: (0, 0)),
                  pl.BlockSpec(lam_vec.shape, lambda b, p, pt: (0, 0)),
                  pl.BlockSpec((1, LANES), lambda b, p, pt: (0, 0))],
        out_specs=pl.BlockSpec((None, rows, LANES), lambda b, p, pt: (b, 0, 0)),
        scratch_shapes=[pltpu.VMEM((8, hw), _BF16), pltpu.VMEM((rows, 1), _F32),
                        pltpu.VMEM((rows, 1), _F32), pltpu.VMEM((rows, LANES), _F32)])
    o = pl.pallas_call(
        functools.partial(_diff_sample_kernel, n_heads=n_heads, steps=steps, page=page,
                          n_pages=n_pages, lam_init=lam_init, scale=(LANES // 2) ** -0.5),
        grid_spec=grid_spec,
        out_shape=jax.ShapeDtypeStruct((db, rows, LANES), _F32),
        compiler_params=_params("arbitrary", "arbitrary"),
        name="diff_attn_sample",
    )(page_table.reshape(-1), q8, cache, new16, slope_rows, lam_vec, sub_gain.reshape(1, LANES))
    o = o.reshape(db, n_heads, 8, LANES)[:, :, :steps]
    return o.transpose(0, 2, 1, 3).reshape(db * steps, hw).astype(_BF16)


def _dilated_prompt_kernel(slopes_ref, *refs, groups, seq_len, heads_per_group, scale):
    ng = len(groups)
    q_refs, k_refs, v_refs = refs[:ng], refs[ng:2 * ng], refs[2 * ng:3 * ng]
    o_ref, og_ref, lse_ref = refs[3 * ng:]
    j = pl.program_id(1)

    for g, (window, dil) in enumerate(groups):
        blk = window // dil
        sub_len = seq_len // dil
        assert sub_len % blk == 0
        nb = sub_len // blk
        slope = slopes_ref[g * heads_per_group + j]
        qi = lax.broadcasted_iota(jnp.int32, (blk, 2 * blk), 0)
        ci = lax.broadcasted_iota(jnp.int32, (blk, 2 * blk), 1)
        sub_dist = qi + blk - ci
        in_band = (sub_dist >= 0) & (sub_dist <= blk)
        bias = -slope * (sub_dist * dil).astype(_F32)
        q_ref, k_ref, v_ref = q_refs[g], k_refs[g], v_refs[g]

        def body(it, _):
            r = it // nb
            n = it % nb
            start = n * (blk * dil) + r
            prev = jnp.maximum(start - blk * dil, r)
            rows_c = pl.ds(start, blk, stride=dil) if dil > 1 else pl.ds(start, blk)
            rows_p = pl.ds(prev, blk, stride=dil) if dil > 1 else pl.ds(prev, blk)
            qb = q_ref[rows_c, :].astype(_BF16)
            kk = jnp.concatenate([k_ref[rows_p, :], k_ref[rows_c, :]], axis=0).astype(_BF16)
            vv = jnp.concatenate([v_ref[rows_p, :], v_ref[rows_c, :]], axis=0).astype(_BF16)
            s = lax.dot_general(qb, kk, _NT, preferred_element_type=_F32) * scale
            valid = in_band & (ci + (n - 1) * blk >= 0)
            s = jnp.where(valid, s + bias, NEG_INF)
            m = jnp.max(s, axis=-1, keepdims=True)
            e = jnp.exp(s - m)
            l = jnp.sum(e, axis=-1, keepdims=True)
            o = jnp.dot(e.astype(_BF16), vv, preferred_element_type=_F32) / l
            og_ref[g, rows_c, :] = o
            lse_ref[g, rows_c, :] = jnp.broadcast_to(m + jnp.log(l), (blk, LANES))
            return 0

        lax.fori_loop(0, dil * nb, body, 0)

    chunk = 256
    for c in range(seq_len // chunk):
        sl = pl.ds(c * chunk, chunk)
        lses = [lse_ref[g, sl, :] for g in range(ng)]
        mx = functools.reduce(jnp.maximum, lses)
        ws = [jnp.exp(x - mx) for x in lses]
        num = functools.reduce(lambda a, b: a + b, [w * og_ref[g, sl, :] for g, w in enumerate(ws)])
        den = functools.reduce(lambda a, b: a + b, ws)
        o_ref[sl, :] = (num / den).astype(o_ref.dtype)


def dilated_attn_prompt(q, kv, bsz, seq_len, groups, heads_per_group):
    ng = len(groups)
    nh = ng * heads_per_group
    slopes = _alibi_slopes(nh)
    blk_spec = lambda off: pl.BlockSpec((seq_len, LANES), lambda b, j, off=off: (b, off + j))
    in_specs = [pl.BlockSpec(memory_space=pltpu.SMEM)]
    in_specs += [blk_spec(g * heads_per_group) for g in range(ng)]
    in_specs += [blk_spec(g * heads_per_group) for g in range(ng)]
    in_specs += [blk_spec(nh + g * heads_per_group) for g in range(ng)]
    return pl.pallas_call(
        functools.partial(_dilated_prompt_kernel, groups=groups, seq_len=seq_len,
                          heads_per_group=heads_per_group, scale=LANES ** -0.5),
        grid=(bsz, heads_per_group),
        in_specs=in_specs,
        out_specs=pl.BlockSpec((seq_len, LANES), lambda b, j: (b, j)),
        out_shape=jax.ShapeDtypeStruct((bsz * seq_len, heads_per_group * LANES), _BF16),
        scratch_shapes=[pltpu.VMEM((ng, seq_len, LANES), _F32), pltpu.VMEM((ng, seq_len, LANES), _F32)],
        compiler_params=_params("arbitrary", "arbitrary"),
        name="dilated_attn_prompt",
    )(slopes, *([q] * ng), *([kv] * ng), *([kv] * ng))


def _dilated_sample_kernel(slopes_ref, q_ref, new_ref, *refs, groups, heads_per_group, steps, scale):
    ng = len(groups)
    st_refs, o_ref = refs[:ng], refs[ng]
    hw = heads_per_group * LANES
    nn = new_ref.shape[0]
    row = lambda shape: lax.broadcasted_iota(jnp.int32, shape, 0)
    col = lambda shape: lax.broadcasted_iota(jnp.int32, shape, 1)

    plans = []
    for g, (window, dil) in enumerate(groups):
        nb = window // dil
        if dil == 1:
            shp = (8, nb)
            back_p = nb + row(shp) - col(shp)
            ok_p = col(shp) >= row(shp)
            back_n = row((8, nn)) - col((8, nn))
            ok_n = back_n >= 0
        else:
            shp = (8, steps * nb)
            back_p = nb - col(shp) % nb
            ok_p = col(shp) // nb == row(shp)
            back_n = jnp.zeros((8, nn), jnp.int32)
            ok_n = row((8, nn)) == col((8, nn))
        plans.append((dil, nb, ok_p, (back_p * dil).astype(_F32), ok_n, (back_n * dil).astype(_F32)))

    for j in range(heads_per_group):
        outs, ms, ls = [], [], []
        for g, (dil, nb, ok_p, back_p, ok_n, back_n) in enumerate(plans):
            slope = slopes_ref[g * heads_per_group + j]
            c0 = (g * heads_per_group + j) * LANES
            qb = q_ref[:, c0:c0 + LANES]
            kn = new_ref[:, c0:c0 + LANES].astype(_BF16)
            vn = new_ref[:, ng * hw + c0:ng * hw + c0 + LANES].astype(_BF16)
            st = st_refs[g]
            if dil == 1:
                kp = st[:, j * LANES:(j + 1) * LANES].astype(_BF16)
                vp = st[:, hw + j * LANES:hw + (j + 1) * LANES].astype(_BF16)
            else:
                kp = jnp.concatenate([st[:, t * 2 * hw + j * LANES:t * 2 * hw + (j + 1) * LANES]
                                      for t in range(steps)], axis=0).astype(_BF16)
                vp = jnp.concatenate([st[:, t * 2 * hw + hw + j * LANES:t * 2 * hw + hw + (j + 1) * LANES]
                                      for t in range(steps)], axis=0).astype(_BF16)
            sp = lax.dot_general(qb, kp, _NT, preferred_element_type=_F32) * scale
            sn = lax.dot_general(qb, kn, _NT, preferred_element_type=_F32) * scale
            sp = jnp.where(ok_p, sp - slope * back_p, NEG_INF)
            sn = jnp.where(ok_n, sn - slope * back_n, NEG_INF)
            m = jnp.maximum(jnp.max(sp, axis=-1, keepdims=True), jnp.max(sn, axis=-1, keepdims=True))
            ep = jnp.exp(sp - m)
            en = jnp.exp(sn - m)
            ls.append(jnp.sum(ep, axis=-1, keepdims=True) + jnp.sum(en, axis=-1, keepdims=True))
            outs.append(jnp.dot(ep.astype(_BF16), vp, preferred_element_type=_F32)
                        + jnp.dot(en.astype(_BF16), vn, preferred_element_type=_F32))
            ms.append(m)
        mx = functools.reduce(jnp.maximum, ms)
        ws = [jnp.exp(m - mx) for m in ms]
        num = functools.reduce(lambda a, b: a + b, [w * o for w, o in zip(ws, outs)])
        den = functools.reduce(lambda a, b: a + b, [w * l for w, l in zip(ws, ls)])
        o_ref[:, j * LANES:(j + 1) * LANES] = num / den


def dilated_attn_sample(q, kv_new, states, layer, groups, heads_per_group, db, steps):
    ng = len(groups)
    hw = heads_per_group * LANES
    nh = ng * heads_per_group
    assert steps <= 8 and all(steps <= dil or dil == 1 for _, dil in groups)
    q8 = jnp.pad(q.reshape(db, steps, nh * LANES), ((0, 0), (0, 8 - steps), (0, 0))).astype(_BF16)
    new16 = jnp.pad(kv_new.reshape(db, steps, 2 * nh * LANES), ((0, 0), (0, 16 - steps), (0, 0)))
    st_views, st_specs = [], []
    for g, (window, dil) in enumerate(groups):
        st = states[g]
        assert st.shape[2] == window
        nb = window // dil
        st_views.append(st.reshape(st.shape[0], db, nb, dil * 2 * hw))
        width = 2 * hw if dil == 1 else steps * 2 * hw
        st_specs.append(pl.BlockSpec((None, None, nb, width), lambda b: (layer, b, 0, 0)))
    o = pl.pallas_call(
        functools.partial(_dilated_sample_kernel, groups=groups, heads_per_group=heads_per_group,
                          steps=steps, scale=LANES ** -0.5),
        grid=(db,),
        in_specs=[pl.BlockSpec(memory_space=pltpu.SMEM),
                  pl.BlockSpec((None, 8, nh * LANES), lambda b: (b, 0, 0)),
                  pl.BlockSpec((None, 16, 2 * nh * LANES), lambda b: (b, 0, 0))] + st_specs,
        out_specs=pl.BlockSpec((None, 8, hw), lambda b: (b, 0, 0)),
        out_shape=jax.ShapeDtypeStruct((db, 8, hw), _F32),
        compiler_params=_params("arbitrary"),
        name="dilated_attn_sample",
    )(_alibi_slopes(nh), q8, new16, *st_views)
    return o[:, :steps].reshape(db * steps, hw).astype(_BF16)


def kernel(x_prompt, x_sample, cache_kv_diff, state_kv_w128, state_kv_w512, state_kv_w2048,
           state_conv, page_table, attn_norm, ffn_norm, a_w_qkv, a_w_o, a_q_gain, a_k_gain,
           a_lambda, a_sub_gain, b_w_qkv, b_w_o, b_q_gain, b_k_gain, ffn_w_gate, ffn_w_up,
           ffn_w_down, ffn_conv_w, ffn_conv_b):
    bsz, seq_len, d_model = x_prompt.shape
    db, steps, _ = x_sample.shape
    depth = attn_norm.shape[0]
    d_ff = ffn_w_gate.shape[2]
    a_heads = a_w_qkv.shape[2] // (3 * LANES)
    ng = len(B_GROUPS)
    b_heads = b_w_qkv.shape[2] // (3 * LANES)
    hg = b_heads // ng
    aw = a_heads * LANES
    bw = b_heads * LANES
    mp, ms = bsz * seq_len, db * steps
    tm, tn = 512, 512
    win_states = (state_kv_w128, state_kv_w512, state_kv_w2048)
    cache = cache_kv_diff.reshape(cache_kv_diff.shape[:3] + (2 * aw,))

    xp = x_prompt.reshape(mp, d_model)
    xs = x_sample.reshape(ms, d_model)
    kv_diff_p, kv_diff_s = [], []
    win_p = [[] for _ in B_GROUPS]
    win_s = [[] for _ in B_GROUPS]
    conv_p, conv_s = [], []

    for i in range(depth):
        hp = rmsnorm_bf16(xp, attn_norm[i], tm)
        hs = rmsnorm_bf16(xs, attn_norm[i], ms)
        if i % 2 == 0:
            a = i // 2
            lam_init = 0.8 - 0.6 * math.exp(-0.3 * i)
            w = a_w_qkv[a]
            qp = matmul_headnorm(hp, w, a_q_gain[a], 0, aw, aw, LANES // 2, _BF16, tm, tn)
            kvp = matmul_headnorm(hp, w, a_k_gain[a], aw, 2 * aw, aw, LANES // 2, _F32, tm, tn)
            qs = matmul_headnorm(hs, w, a_q_gain[a], 0, aw, aw, LANES // 2, _BF16, ms, tn)
            kvs = matmul_headnorm(hs, w, a_k_gain[a], aw, 2 * aw, aw, LANES // 2, _F32, ms, tn)
            op = diff_attn_prompt(qp, kvp, a_lambda[a], a_sub_gain[a], lam_init, bsz, seq_len,
                                  a_heads, 256)
            os_ = diff_attn_sample(qs, kvs, cache, a, page_table, a_lambda[a], a_sub_gain[a],
                                   lam_init, a_heads, steps)
            xp = matmul_residual(op, a_w_o[a], xp, tm, tn)
            xs = matmul_residual(os_, a_w_o[a], xs, ms, tn)
            kv_diff_p.append(kvp.reshape(bsz, seq_len, 2, a_heads, LANES))
            kv_diff_s.append(kvs.reshape(db, steps, 2, a_heads, LANES))
        else:
            b = i // 2
            w = b_w_qkv[b]
            qp = matmul_headnorm(hp, w, b_q_gain[b], 0, bw, bw, LANES, _F32, tm, tn)
            kvp = matmul_headnorm(hp, w, b_k_gain[b], bw, 2 * bw, bw, LANES, _F32, tm, tn)
            qs = matmul_headnorm(hs, w, b_q_gain[b], 0, bw, bw, LANES, _F32, ms, tn)
            kvs = matmul_headnorm(hs, w, b_k_gain[b], bw, 2 * bw, bw, LANES, _F32, ms, tn)
            op = dilated_attn_prompt(qp, kvp, bsz, seq_len, B_GROUPS, hg)
            os_ = dilated_attn_sample(qs, kvs, win_states, b, B_GROUPS, hg, db, steps)
            xp = matmul_residual(op, b_w_o[b], xp, tm, tn)
            xs = matmul_residual(os_, b_w_o[b], xs, ms, tn)
            kp4 = kvp.reshape(bsz, seq_len, 2, ng, hg, LANES)
            ks4 = kvs.reshape(db, steps, 2, ng, hg, LANES)
            for g, (window, _) in enumerate(B_GROUPS):
                keep = min(window, seq_len)
                win_p[g].append(kp4[:, seq_len - keep:, :, g])
                win_s[g].append(jnp.concatenate([win_states[g][b][:, steps:], ks4[:, :, :, g]], axis=1))
        hp = rmsnorm_bf16(xp, ffn_norm[i], tm)
        hs = rmsnorm_bf16(xs, ffn_norm[i], ms)
        actp, tail = ffn_gate_up_prompt(hp, ffn_w_gate[i], ffn_w_up[i], ffn_conv_w[i], ffn_conv_b[i],
                                        seq_len, tm, 256)
        acts, gs = ffn_gate_up_sample(hs, ffn_w_gate[i], ffn_w_up[i], ffn_conv_w[i], ffn_conv_b[i],
                                      state_conv[i], 256)
        xp = matmul_residual(actp, ffn_w_down[i], xp, tm, 256)
        xs = matmul_residual(acts, ffn_w_down[i], xs, ms, 256)
        conv_p.append(tail[:, 6:])
        conv_s.append(gs.reshape(db, steps, d_ff)[:, steps - 2:])

    return (xp.reshape(bsz, seq_len, d_model), xs.reshape(db, steps, d_model),
            jnp.stack(kv_diff_p), jnp.stack(kv_diff_s),
            jnp.stack(win_p[0]), jnp.stack(win_s[0]), jnp.stack(win_p[1]), jnp.stack(win_s[1]),
            jnp.stack(win_p[2]), jnp.stack(win_s[2]), jnp.stack(conv_p), jnp.stack(conv_s))
```

```python
import functools
import math

import jax
import jax.numpy as jnp
from jax import lax
from jax.experimental import pallas as pl
from jax.experimental.pallas import tpu as pltpu

EPS = 1e-6
NEG_INF = -1e30
LANES = 128
MXU_COLS = 256
B_GROUPS = ((128, 1), (512, 4), (2048, 16))
VMEM_LIMIT = 56 * 1024 * 1024

_BF16 = jnp.bfloat16
_F32 = jnp.float32
_NT = (((1,), (1,)), ((), ()))


def _params(*sem):
    return pltpu.CompilerParams(dimension_semantics=sem, vmem_limit_bytes=VMEM_LIMIT)


def _alibi_slopes(n):
    return 2.0 ** (-8.0 * jnp.arange(1, n + 1, dtype=_F32) / n)


def _rmsnorm_kernel(x_ref, g_ref, o_ref):
    x = x_ref[...]
    inv = lax.rsqrt(jnp.mean(x * x, axis=-1, keepdims=True) + EPS)
    o_ref[...] = (x * inv * g_ref[...]).astype(o_ref.dtype)


def rmsnorm_bf16(x, gain, tm):
    m, d = x.shape
    return pl.pallas_call(
        _rmsnorm_kernel,
        grid=(pl.cdiv(m, tm),),
        in_specs=[pl.BlockSpec((tm, d), lambda i: (i, 0)),
                  pl.BlockSpec((1, d), lambda i: (0, 0))],
        out_specs=pl.BlockSpec((tm, d), lambda i: (i, 0)),
        out_shape=jax.ShapeDtypeStruct((m, d), _BF16),
        compiler_params=_params("parallel"),
        name="rmsnorm",
    )(x, gain.reshape(1, d))


def _cast_weight(w_ref, wb_ref):
    @pl.when(pl.program_id(1) == 0)
    def _():
        wb_ref[...] = w_ref[...].astype(_BF16)


def _col_chunks(tn):
    step = min(tn, MXU_COLS)
    return [slice(c, c + step) for c in range(0, tn, step)]


def _mm_res_kernel(a_ref, w_ref, r_ref, o_ref, wb_ref):
    _cast_weight(w_ref, wb_ref)
    for sl in _col_chunks(o_ref.shape[1]):
        acc = jnp.dot(a_ref[...], wb_ref[:, sl], preferred_element_type=_F32)
        o_ref[:, sl] = r_ref[:, sl] + acc


def matmul_residual(a, w, res, tm, tn):
    m, k = a.shape
    n = w.shape[1]
    return pl.pallas_call(
        _mm_res_kernel,
        grid=(n // tn, pl.cdiv(m, tm)),
        in_specs=[pl.BlockSpec((tm, k), lambda j, i: (i, 0)),
                  pl.BlockSpec((k, tn), lambda j, i: (0, j)),
                  pl.BlockSpec((tm, tn), lambda j, i: (i, j))],
        out_specs=pl.BlockSpec((tm, tn), lambda j, i: (i, j)),
        out_shape=jax.ShapeDtypeStruct((m, n), _F32),
        scratch_shapes=[pltpu.VMEM((k, tn), _BF16)],
        compiler_params=_params("arbitrary", "arbitrary"),
        name="matmul_residual",
    )(a, w, res)


def _head_rms(c, gain, group):
    sq = c * c
    if group == LANES:
        ms = jnp.sum(sq, axis=-1, keepdims=True) * (1.0 / group)
    else:
        lane = lax.broadcasted_iota(jnp.int32, c.shape, 1)
        low = lane < group
        s_lo = jnp.sum(jnp.where(low, sq, 0.0), axis=-1, keepdims=True)
        s_hi = jnp.sum(jnp.where(low, 0.0, sq), axis=-1, keepdims=True)
        ms = jnp.where(low, s_lo, s_hi) * (1.0 / group)
    return c * lax.rsqrt(ms + EPS) * gain


def _mm_norm_kernel(a_ref, w_ref, g_ref, *rest, group, norm_blocks, n_blocks):
    o_ref, wb_ref = rest[-2:]
    _cast_weight(w_ref, wb_ref)
    chunks = _col_chunks(o_ref.shape[1])

    def normed():
        g = g_ref[...]
        for sl in chunks:
            acc = jnp.dot(a_ref[...], wb_ref[:, sl], preferred_element_type=_F32)
            for c in range(sl.start, sl.stop, LANES):
                o_ref[:, c:c + LANES] = _head_rms(acc[:, c - sl.start:c - sl.start + LANES],
                                                  g, group).astype(o_ref.dtype)

    def raw():
        for sl in chunks:
            acc = jnp.dot(a_ref[...], wb_ref[:, sl], preferred_element_type=_F32)
            o_ref[:, sl] = acc.astype(o_ref.dtype)

    if norm_blocks >= n_blocks:
        normed()
    elif norm_blocks == 0:
        raw()
    else:
        pl.when(pl.program_id(0) < norm_blocks)(normed)
        pl.when(pl.program_id(0) >= norm_blocks)(raw)


def matmul_headnorm(a, w, gain, col0, ncols, norm_cols, group, out_dtype, tm, tn, stack=None):
    m, k = a.shape
    assert col0 % tn == 0 and ncols % tn == 0 and norm_cols % tn == 0
    assert LANES % group == 0 and tn % LANES == 0
    off = col0 // tn
    n_blocks = ncols // tn
    g = jnp.tile(gain.astype(_F32), LANES // group).reshape(1, LANES)
    in_specs = [pl.BlockSpec((tm, k), lambda j, i: (i, 0)),
                pl.BlockSpec((k, tn), lambda j, i: (0, j + off)),
                pl.BlockSpec((1, LANES), lambda j, i: (0, 0))]
    args = [a, w, g]
    aliases = {}
    if stack is None:
        out_spec = pl.BlockSpec((tm, tn), lambda j, i: (i, j))
        out_shape = jax.ShapeDtypeStruct((m, ncols), out_dtype)
    else:
        buf, layers, li = stack
        out_spec = pl.BlockSpec((None, tm, tn), lambda j, i: (li, i, j))
        out_shape = jax.ShapeDtypeStruct((layers, m, ncols), out_dtype)
        if buf is not None:
            in_specs.append(pl.BlockSpec(memory_space=pl.ANY))
            args.append(buf)
            aliases = {3: 0}
    return pl.pallas_call(
        functools.partial(_mm_norm_kernel, group=group, norm_blocks=norm_cols // tn,
                          n_blocks=n_blocks),
        grid=(n_blocks, pl.cdiv(m, tm)),
        in_specs=in_specs,
        out_specs=out_spec,
        out_shape=out_shape,
        input_output_aliases=aliases,
        scratch_shapes=[pltpu.VMEM((k, tn), _BF16)],
        compiler_params=_params("arbitrary", "arbitrary"),
        name="matmul_headnorm",
    )(*args)


def _gated(g, g1, g2, u, cw, cb):
    c = cb + g2 * cw[0:1] + g1 * cw[1:2] + g * cw[2:3]
    return (c * jax.nn.sigmoid(c) * u).astype(_BF16)


def _ffn_prompt_kernel(a_ref, wg_ref, wu_ref, cw_ref, cb_ref, act_ref, tail_ref,
                       wgb_ref, wub_ref, carry_ref, *, tiles_per_seq):
    _cast_weight(wg_ref, wgb_ref)
    _cast_weight(wu_ref, wub_ref)
    tm = a_ref.shape[0]

    @pl.when(pl.program_id(1) % tiles_per_seq == 0)
    def _():
        carry_ref[...] = jnp.zeros_like(carry_ref)

    for sl in _col_chunks(act_ref.shape[1]):
        g = jnp.dot(a_ref[...], wgb_ref[:, sl], preferred_element_type=_F32)
        u = jnp.dot(a_ref[...], wub_ref[:, sl], preferred_element_type=_F32)
        prev = carry_ref[:, sl]
        row = lax.broadcasted_iota(jnp.int32, g.shape, 0)
        g1 = jnp.where(row == 0, prev[7:8], pltpu.roll(g, 1, axis=0))
        g2 = jnp.where(row == 0, prev[6:7], jnp.where(row == 1, prev[7:8], pltpu.roll(g, 2, axis=0)))
        act_ref[:, sl] = _gated(g, g1, g2, u, cw_ref[:, sl], cb_ref[:, sl])
        tail = g[tm - 8:tm]
        carry_ref[:, sl] = tail
        tail_ref[:, sl] = tail


def ffn_gate_up_prompt(a, w_gate, w_up, conv_w, conv_b, seq_len, tm, tn):
    m, k = a.shape
    f = w_gate.shape[1]
    assert seq_len % tm == 0 and m % seq_len == 0 and f % tn == 0
    tps = seq_len // tm
    return pl.pallas_call(
        functools.partial(_ffn_prompt_kernel, tiles_per_seq=tps),
        grid=(f // tn, m // tm),
        in_specs=[pl.BlockSpec((tm, k), lambda j, i: (i, 0)),
                  pl.BlockSpec((k, tn), lambda j, i: (0, j)),
                  pl.BlockSpec((k, tn), lambda j, i: (0, j)),
                  pl.BlockSpec((3, tn), lambda j, i: (0, j)),
                  pl.BlockSpec((1, tn), lambda j, i: (0, j))],
        out_specs=[pl.BlockSpec((tm, tn), lambda j, i: (i, j)),
                   pl.BlockSpec((None, 8, tn), lambda j, i: (i // tps, 0, j))],
        out_shape=[jax.ShapeDtypeStruct((m, f), _BF16),
                   jax.ShapeDtypeStruct((m // seq_len, 8, f), _F32)],
        scratch_shapes=[pltpu.VMEM((k, tn), _BF16), pltpu.VMEM((k, tn), _BF16),
                        pltpu.VMEM((8, tn), _F32)],
        compiler_params=_params("arbitrary", "arbitrary"),
        name="ffn_gate_up_prompt",
    )(a, w_gate, w_up, conv_w, conv_b.reshape(1, f))


def _ffn_sample_kernel(a_ref, wg_ref, wu_ref, cw_ref, cb_ref, s0_ref, s1_ref, act_ref, g_ref,
                       *, steps):
    a = a_ref[...]
    g = jnp.dot(a, wg_ref[...].astype(_BF16), preferred_element_type=_F32)
    u = jnp.dot(a, wu_ref[...].astype(_BF16), preferred_element_type=_F32)
    pos = lax.broadcasted_iota(jnp.int32, g.shape, 0) % steps
    s0 = s0_ref[...]
    s1 = s1_ref[...]
    g1 = jnp.where(pos == 0, s1, pltpu.roll(g, 1, axis=0))
    g2 = jnp.where(pos == 0, s0, jnp.where(pos == 1, s1, pltpu.roll(g, 2, axis=0)))
    act_ref[...] = _gated(g, g1, g2, u, cw_ref[...], cb_ref[...])
    g_ref[...] = g


def ffn_gate_up_sample(a, w_gate, w_up, conv_w, conv_b, state, tn):
    m, k = a.shape
    f = w_gate.shape[1]
    db = state.shape[0]
    steps = m // db
    assert steps >= 2 and f % tn == 0
    s0 = jnp.repeat(state[:, 0], steps, axis=0)
    s1 = jnp.repeat(state[:, 1], steps, axis=0)
    col = lambda j: (0, j)
    return pl.pallas_call(
        functools.partial(_ffn_sample_kernel, steps=steps),
        grid=(f // tn,),
        in_specs=[pl.BlockSpec((m, k), lambda j: (0, 0)),
                  pl.BlockSpec((k, tn), col), pl.BlockSpec((k, tn), col),
                  pl.BlockSpec((3, tn), col), pl.BlockSpec((1, tn), col),
                  pl.BlockSpec((m, tn), col), pl.BlockSpec((m, tn), col)],
        out_specs=[pl.BlockSpec((m, tn), col), pl.BlockSpec((m, tn), col)],
        out_shape=[jax.ShapeDtypeStruct((m, f), _BF16), jax.ShapeDtypeStruct((m, f), _F32)],
        compiler_params=_params("arbitrary"),
        name="ffn_gate_up_sample",
    )(a, w_gate, w_up, conv_w, conv_b.reshape(1, f), s0, s1)


def _diff_lambda(lam_ref, lam_init):
    lv = lam_ref[...]
    a = jnp.sum(lv[0:1] * lv[1:2], axis=-1, keepdims=True)
    b = jnp.sum(lv[2:3] * lv[3:4], axis=-1, keepdims=True)
    return jnp.exp(a) - jnp.exp(b) + lam_init


def _sub_norm(o, sg_ref, lam_init):
    inv = lax.rsqrt(jnp.mean(o * o, axis=-1, keepdims=True) + EPS)
    return o * inv * sg_ref[...] * (1.0 - lam_init)


def _split_sub_heads(q, group_rows, rows_first_map):
    lane = lax.broadcasted_iota(jnp.int32, q.shape, 1)
    row = lax.broadcasted_iota(jnp.int32, q.shape, 0) % group_rows
    keep = (lane < LANES // 2) == (row < rows_first_map)
    return jnp.where(keep, q, jnp.zeros_like(q))


def _diff_prompt_kernel(slopes_ref, lam_ref, q_ref, k_ref, v_ref, sg_ref, o_ref,
                        kb_ref, vt_ref, rb_ref, rbd_ref, m_ref, l_ref, acc_ref,
                        *, tq, lam_init, scale):
    h = pl.program_id(1)
    qi = pl.program_id(2)
    slope = slopes_ref[h]

    @pl.when(qi == 0)
    def _():
        kb_ref[...] = k_ref[...].astype(_BF16)
        for jb in range(vt_ref.shape[0]):
            vt_ref[jb] = v_ref[jb * tq:(jb + 1) * tq, :].T.astype(_BF16)
        rel = (lax.broadcasted_iota(jnp.int32, (tq, tq), 1)
               - lax.broadcasted_iota(jnp.int32, (tq, tq), 0))
        rb = -slope * rel.astype(_F32)
        rb_ref[...] = rb
        rbd_ref[...] = jnp.where(rel >= 0, rb, NEG_INF)

    qs = (q_ref[...].astype(_F32) * scale).astype(_BF16)
    lane = lax.broadcasted_iota(jnp.int32, qs.shape, 1)
    zero = jnp.zeros_like(qs)
    q_maps = (jnp.where(lane < LANES // 2, qs, zero), jnp.where(lane < LANES // 2, zero, qs))
    q0 = qi * tq

    m_ref[...] = jnp.full_like(m_ref, NEG_INF)
    l_ref[...] = jnp.zeros_like(l_ref)
    acc_ref[...] = jnp.zeros_like(acc_ref)

    def step(kb, vt, bias, shift):
        for mp, qm in enumerate(q_maps):
            s = lax.dot_general(kb, qm, _NT, preferred_element_type=_F32) + bias
            m = m_ref[mp]
            m_new = jnp.maximum(m, jnp.max(s, axis=0, keepdims=True) - shift)
            alpha = jnp.exp(m - m_new)
            p = jnp.exp(s - (m_new + shift))
            l_ref[mp] = alpha * l_ref[mp] + jnp.sum(p, axis=0, keepdims=True)
            m_ref[mp] = m_new
            acc_ref[mp] = alpha * acc_ref[mp] + jnp.dot(vt, p.astype(_BF16),
                                                        preferred_element_type=_F32)

    def body(j, carry):
        k0 = pl.multiple_of(j * tq, tq)
        shift = slope * (q0 - k0).astype(_F32)
        step(kb_ref[pl.ds(k0, tq), :], vt_ref[j], rb_ref[...], shift)
        return carry

    lax.fori_loop(0, qi, body, 0)
    kd = pl.multiple_of(q0, tq)
    step(kb_ref[pl.ds(kd, tq), :], vt_ref[qi], rbd_ref[...], 0.0)
    lam = _diff_lambda(lam_ref, lam_init)
    od = acc_ref[0] / l_ref[0] - lam * (acc_ref[1] / l_ref[1])
    inv = lax.rsqrt(jnp.mean(od * od, axis=0, keepdims=True) + EPS)
    o_ref[...] = ((od * inv).T * sg_ref[...] * (1.0 - lam_init)).astype(o_ref.dtype)


def diff_attn_prompt(q, kv, li, lam_vec, sub_gain, lam_init, bsz, seq_len, n_heads, tq):
    nq = seq_len // tq
    slopes = _alibi_slopes(n_heads)
    return pl.pallas_call(
        functools.partial(_diff_prompt_kernel, tq=tq, lam_init=lam_init,
                          scale=(LANES // 2) ** -0.5),
        grid=(bsz, n_heads, nq),
        in_specs=[pl.BlockSpec(memory_space=pltpu.SMEM),
                  pl.BlockSpec(lam_vec.shape, lambda b, h, i: (0, 0)),
                  pl.BlockSpec((tq, LANES), lambda b, h, i: (b * nq + i, h)),
                  pl.BlockSpec((None, seq_len, LANES), lambda b, h, i: (li, b, h)),
                  pl.BlockSpec((None, seq_len, LANES), lambda b, h, i: (li, b, n_heads + h)),
                  pl.BlockSpec((1, LANES), lambda b, h, i: (0, 0))],
        out_specs=pl.BlockSpec((tq, LANES), lambda b, h, i: (b * nq + i, h)),
        out_shape=jax.ShapeDtypeStruct((bsz * seq_len, n_heads * LANES), _BF16),
        scratch_shapes=[pltpu.VMEM((seq_len, LANES), _BF16), pltpu.VMEM((nq, LANES, tq), _BF16),
                        pltpu.VMEM((tq, tq), _F32), pltpu.VMEM((tq, tq), _F32),
                        pltpu.VMEM((2, 1, tq), _F32), pltpu.VMEM((2, 1, tq), _F32),
                        pltpu.VMEM((2, LANES, tq), _F32)],
        compiler_params=_params("arbitrary", "arbitrary", "arbitrary"),
        name="diff_attn_prompt",
    )(slopes, lam_vec, q, kv, kv, sub_gain.reshape(1, LANES))


def _diff_sample_kernel(pt_ref, q_ref, bias_ref, biasn_ref, slope_ref, lam_ref, sg_ref, new_ref,
                        *rest, n_heads, steps, page, n_pages, pages_per_step, lam_init, scale):
    del pt_ref
    page_refs = rest[:pages_per_step]
    o_ref, qs_ref, m_ref, l_ref, acc_ref = rest[pages_per_step:]
    p = pl.program_id(1)
    rows = 2 * steps * n_heads

    @pl.when(p == 0)
    def _():
        q = _split_sub_heads(q_ref[...], 2 * steps, steps)
        qs_ref[...] = (q.astype(_F32) * scale).astype(_BF16)
        m_ref[...] = jnp.full_like(m_ref, NEG_INF)
        l_ref[...] = jnp.zeros_like(l_ref)
        acc_ref[...] = jnp.zeros_like(acc_ref)

    def attend(kv_ref, bias, shift):
        nk = kv_ref.shape[0] * n_heads
        k = kv_ref[:, 0].reshape(nk, LANES).astype(_BF16)
        v = kv_ref[:, 1].reshape(nk, LANES).astype(_BF16)
        s = lax.dot_general(qs_ref[...], k, _NT, preferred_element_type=_F32) + bias
        m = m_ref[...]
        m_new = jnp.maximum(m, jnp.max(s, axis=-1, keepdims=True) - shift)
        alpha = jnp.exp(m - m_new)
        e = jnp.exp(s - (m_new + shift))
        l_ref[...] = alpha * l_ref[...] + jnp.sum(e, axis=-1, keepdims=True)
        m_ref[...] = m_new
        acc_ref[...] = alpha * acc_ref[...] + jnp.dot(e.astype(_BF16), v, preferred_element_type=_F32)

    slope = slope_ref[...]
    for i, page_ref in enumerate(page_refs):
        first = (n_pages - (p * pages_per_step + i)) * page
        attend(page_ref, bias_ref[...], slope * first.astype(_F32))

    @pl.when(p == n_pages // pages_per_step - 1)
    def _():
        attend(new_ref, biasn_ref[...], 0.0)
        o = acc_ref[...] / l_ref[...]
        lam = _diff_lambda(lam_ref, lam_init)
        od = o - lam * pltpu.roll(o, rows - steps, axis=0)
        o_ref[...] = _sub_norm(od, sg_ref, lam_init)


def diff_attn_sample(q, kv_new, cache, layer, page_table, lam_vec, sub_gain, lam_init,
                     n_heads, steps, pages_per_step):
    db, n_pages = page_table.shape
    page = cache.shape[2]
    hw = n_heads * LANES
    rows = 2 * steps * n_heads
    new_keys = 16
    assert rows % 8 == 0 and n_pages % pages_per_step == 0 and steps <= new_keys
    qr = q.reshape(db, steps, n_heads, 1, LANES).transpose(0, 2, 3, 1, 4)
    qr = jnp.broadcast_to(qr, (db, n_heads, 2, steps, LANES)).reshape(db, rows, LANES)
    new = jnp.pad(kv_new.reshape(db, steps, 2, n_heads, LANES),
                  ((0, 0), (0, new_keys - steps), (0, 0), (0, 0), (0, 0)))
    slope_rows = jnp.repeat(_alibi_slopes(n_heads), 2 * steps).reshape(rows, 1)

    def bias_table(n_keys, causal):
        r = jnp.arange(rows)[:, None]
        c = jnp.arange(n_keys * n_heads)[None, :]
        back = r % steps - c // n_heads
        ok = r // (2 * steps) == c % n_heads
        if causal:
            ok = ok & (back >= 0)
        return jnp.where(ok, -slope_rows * back.astype(_F32), NEG_INF)

    page_spec = lambda i: pl.BlockSpec(
        (None, None, page, 2, n_heads, LANES),
        lambda b, p, pt: (layer, pt[b * n_pages + p * pages_per_step + i], 0, 0, 0, 0))
    const = lambda shape: pl.BlockSpec(shape, lambda b, p, pt: (0,) * len(shape))
    grid_spec = pltpu.PrefetchScalarGridSpec(
        num_scalar_prefetch=1,
        grid=(db, n_pages // pages_per_step),
        in_specs=[pl.BlockSpec((None, rows, LANES), lambda b, p, pt: (b, 0, 0)),
                  const((rows, page * n_heads)), const((rows, new_keys * n_heads)),
                  const((rows, 1)), const(lam_vec.shape), const((1, LANES)),
                  pl.BlockSpec((None, new_keys, 2, n_heads, LANES), lambda b, p, pt: (b, 0, 0, 0, 0))]
                 + [page_spec(i) for i in range(pages_per_step)],
        out_specs=pl.BlockSpec((None, rows, LANES), lambda b, p, pt: (b, 0, 0)),
        scratch_shapes=[pltpu.VMEM((rows, LANES), _BF16), pltpu.VMEM((rows, 1), _F32),
                        pltpu.VMEM((rows, 1), _F32), pltpu.VMEM((rows, LANES), _F32)])
    o = pl.pallas_call(
        functools.partial(_diff_sample_kernel, n_heads=n_heads, steps=steps, page=page,
                          n_pages=n_pages, pages_per_step=pages_per_step, lam_init=lam_init,
                          scale=(LANES // 2) ** -0.5),
        grid_spec=grid_spec,
        out_shape=jax.ShapeDtypeStruct((db, rows, LANES), _F32),
        compiler_params=_params("arbitrary", "arbitrary"),
        name="diff_attn_sample",
    )(page_table.reshape(-1), qr, bias_table(page, False), bias_table(new_keys, True), slope_rows,
      lam_vec, sub_gain.reshape(1, LANES), new, *([cache] * pages_per_step))
    o = o.reshape(db, n_heads, 2 * steps, LANES)[:, :, :steps]
    return o.transpose(0, 2, 1, 3).reshape(db * steps, hw).astype(_BF16)


def _dilated_prompt_kernel(slopes_ref, *refs, groups, seq_len, heads_per_group, scale):
    ng = len(groups)
    q_refs, k_refs, v_refs = refs[:ng], refs[ng:2 * ng], refs[2 * ng:3 * ng]
    o_ref, og_ref, lse_ref = refs[3 * ng:]
    j = pl.program_id(1)

    for g, (window, dil) in enumerate(groups):
        blk = window // dil
        sub_len = seq_len // dil
        assert sub_len % blk == 0
        nb = sub_len // blk
        slope = slopes_ref[g * heads_per_group + j]
        qi = lax.broadcasted_iota(jnp.int32, (blk, 2 * blk), 0)
        ci = lax.broadcasted_iota(jnp.int32, (blk, 2 * blk), 1)
        sub_dist = qi + blk - ci
        in_band = (sub_dist >= 0) & (sub_dist <= blk)
        bias = -slope * (sub_dist * dil).astype(_F32)
        q_ref, k_ref, v_ref = q_refs[g], k_refs[g], v_refs[g]

        def body(it, _):
            r = it // nb
            n = it % nb
            start = n * (blk * dil) + r
            prev = jnp.maximum(start - blk * dil, r)
            rows_c = pl.ds(start, blk, stride=dil) if dil > 1 else pl.ds(start, blk)
            rows_p = pl.ds(prev, blk, stride=dil) if dil > 1 else pl.ds(prev, blk)
            qb = q_ref[rows_c, :].astype(_BF16)
            kk = jnp.concatenate([k_ref[rows_p, :], k_ref[rows_c, :]], axis=0).astype(_BF16)
            vv = jnp.concatenate([v_ref[rows_p, :], v_ref[rows_c, :]], axis=0).astype(_BF16)
            s = lax.dot_general(qb, kk, _NT, preferred_element_type=_F32) * scale
            valid = in_band & (ci + (n - 1) * blk >= 0)
            s = jnp.where(valid, s + bias, NEG_INF)
            m = jnp.max(s, axis=-1, keepdims=True)
            e = jnp.exp(s - m)
            l = jnp.sum(e, axis=-1, keepdims=True)
            o = jnp.dot(e.astype(_BF16), vv, preferred_element_type=_F32) / l
            og_ref[g, rows_c, :] = o
            lse_ref[g, rows_c, :] = jnp.broadcast_to(m + jnp.log(l), (blk, LANES))
            return 0

        lax.fori_loop(0, dil * nb, body, 0, unroll=2)

    chunk = 256
    for c in range(seq_len // chunk):
        sl = pl.ds(c * chunk, chunk)
        lses = [lse_ref[g, sl, :] for g in range(ng)]
        mx = functools.reduce(jnp.maximum, lses)
        ws = [jnp.exp(x - mx) for x in lses]
        num = functools.reduce(lambda a, b: a + b, [w * og_ref[g, sl, :] for g, w in enumerate(ws)])
        den = functools.reduce(lambda a, b: a + b, ws)
        o_ref[sl, :] = (num / den).astype(o_ref.dtype)


def dilated_attn_prompt(q, kv, bsz, seq_len, groups, heads_per_group):
    ng = len(groups)
    nh = ng * heads_per_group
    slopes = _alibi_slopes(nh)
    blk_spec = lambda off: pl.BlockSpec((seq_len, LANES), lambda b, j, off=off: (b, off + j))
    in_specs = [pl.BlockSpec(memory_space=pltpu.SMEM)]
    in_specs += [blk_spec(g * heads_per_group) for g in range(ng)]
    in_specs += [blk_spec(g * heads_per_group) for g in range(ng)]
    in_specs += [blk_spec(nh + g * heads_per_group) for g in range(ng)]
    return pl.pallas_call(
        functools.partial(_dilated_prompt_kernel, groups=groups, seq_len=seq_len,
                          heads_per_group=heads_per_group, scale=LANES ** -0.5),
        grid=(bsz, heads_per_group),
        in_specs=in_specs,
        out_specs=pl.BlockSpec((seq_len, LANES), lambda b, j: (b, j)),
        out_shape=jax.ShapeDtypeStruct((bsz * seq_len, heads_per_group * LANES), _BF16),
        scratch_shapes=[pltpu.VMEM((ng, seq_len, LANES), _F32), pltpu.VMEM((ng, seq_len, LANES), _F32)],
        compiler_params=_params("arbitrary", "arbitrary"),
        name="dilated_attn_prompt",
    )(slopes, *([q] * ng), *([kv] * ng), *([kv] * ng))


def _dilated_sample_kernel(q_ref, new_ref, *refs, groups, steps, scale):
    ng = len(groups)
    st_refs, bp_refs, bn_refs = refs[:ng], refs[ng:2 * ng], refs[2 * ng:3 * ng]
    o_ref = refs[3 * ng]
    for t in range(steps):
        ms, ls, accs = [], [], []
        for g, (_, dil) in enumerate(groups):
            sub = t if dil > 1 else 0
            qb = q_ref[t, g].astype(_BF16)
            flat = lambda x: x.reshape(x.shape[0] * x.shape[1], LANES).astype(_BF16)
            kp, vp = flat(st_refs[g][:, sub, 0]), flat(st_refs[g][:, sub, 1])
            kn, vn = flat(new_ref[:, 0, g]), flat(new_ref[:, 1, g])
            sp = lax.dot_general(qb, kp, _NT, preferred_element_type=_F32) * scale + bp_refs[g][t]
            sn = lax.dot_general(qb, kn, _NT, preferred_element_type=_F32) * scale + bn_refs[g][t]
            m = jnp.maximum(jnp.max(sp, axis=-1, keepdims=True), jnp.max(sn, axis=-1, keepdims=True))
            ep = jnp.exp(sp - m)
            en = jnp.exp(sn - m)
            ls.append(jnp.sum(ep, axis=-1, keepdims=True) + jnp.sum(en, axis=-1, keepdims=True))
            accs.append(jnp.dot(ep.astype(_BF16), vp, preferred_element_type=_F32)
                        + jnp.dot(en.astype(_BF16), vn, preferred_element_type=_F32))
            ms.append(m)
        mx = functools.reduce(jnp.maximum, ms)
        ws = [jnp.exp(m - mx) for m in ms]
        num = functools.reduce(lambda a, b: a + b, [w * o for w, o in zip(ws, accs)])
        den = functools.reduce(lambda a, b: a + b, [w * l for w, l in zip(ws, ls)])
        o_ref[t] = num / den


def dilated_attn_sample(q, kv_new, states, layer, groups, heads_per_group, db, steps):
    ng = len(groups)
    hg = heads_per_group
    new_keys = 16
    assert steps <= new_keys and all(steps <= dil or dil == 1 for _, dil in groups)
    q5 = q.reshape(db, steps, ng, hg, LANES)
    new = jnp.pad(kv_new.reshape(db, steps, 2, ng, hg, LANES),
                  ((0, 0), (0, new_keys - steps)) + ((0, 0),) * 4)
    slopes = _alibi_slopes(ng * hg).reshape(ng, hg, 1)
    head = jnp.arange(hg)[:, None]
    t = jnp.arange(steps)[:, None, None]
    views, st_specs, bias_p, bias_n = [], [], [], []
    for g, (window, dil) in enumerate(groups):
        st = states[g]
        assert st.shape[2] == window
        nb = window // dil
        views.append(st.reshape(st.shape[0], db, nb, dil, 2, hg, LANES))
        st_specs.append(pl.BlockSpec((None, None, nb, min(dil, steps), 2, hg, LANES),
                                     lambda b: (layer, b, 0, 0, 0, 0, 0)))
        col = jnp.arange(nb * hg)[None, :]
        m, hp = col // hg, col % hg
        coln = jnp.arange(new_keys * hg)[None, :]
        tn, hn = coln // hg, coln % hg
        if dil == 1:
            back_p, ok_p = nb + t - m, (hp == head) & (m >= t)
            back_n, ok_n = t - tn, (hn == head) & (tn <= t)
        else:
            back_p, ok_p = nb - m + 0 * t, (hp == head) & (t >= 0)
            back_n, ok_n = 0 * (t - tn), (hn == head) & (tn == t)
        bias_p.append(jnp.where(ok_p, -slopes[g] * (back_p * dil).astype(_F32), NEG_INF))
        bias_n.append(jnp.where(ok_n, -slopes[g] * (back_n * dil).astype(_F32), NEG_INF))
    full = lambda x: pl.BlockSpec(x.shape, lambda b: (0,) * x.ndim)
    o = pl.pallas_call(
        functools.partial(_dilated_sample_kernel, groups=groups, steps=steps, scale=LANES ** -0.5),
        grid=(db,),
        in_specs=[pl.BlockSpec((None, steps, ng, hg, LANES), lambda b: (b, 0, 0, 0, 0)),
                  pl.BlockSpec((None, new_keys, 2, ng, hg, LANES), lambda b: (b, 0, 0, 0, 0, 0))]
                 + st_specs + [full(x) for x in bias_p] + [full(x) for x in bias_n],
        out_specs=pl.BlockSpec((None, steps, hg, LANES), lambda b: (b, 0, 0, 0)),
        out_shape=jax.ShapeDtypeStruct((db, steps, hg, LANES), _F32),
        compiler_params=_params("arbitrary"),
        name="dilated_attn_sample",
    )(q5, new, *views, *bias_p, *bias_n)
    return o.reshape(db * steps, hg * LANES).astype(_BF16)


def kernel(x_prompt, x_sample, cache_kv_diff, state_kv_w128, state_kv_w512, state_kv_w2048,
           state_conv, page_table, attn_norm, ffn_norm, a_w_qkv, a_w_o, a_q_gain, a_k_gain,
           a_lambda, a_sub_gain, b_w_qkv, b_w_o, b_q_gain, b_k_gain, ffn_w_gate, ffn_w_up,
           ffn_w_down, ffn_conv_w, ffn_conv_b):
    bsz, seq_len, d_model = x_prompt.shape
    db, steps, _ = x_sample.shape
    depth = attn_norm.shape[0]
    d_ff = ffn_w_gate.shape[2]
    a_heads = a_w_qkv.shape[2] // (3 * LANES)
    ng = len(B_GROUPS)
    b_heads = b_w_qkv.shape[2] // (3 * LANES)
    hg = b_heads // ng
    aw = a_heads * LANES
    bw = b_heads * LANES
    mp, ms = bsz * seq_len, db * steps
    tm, tn = min(1024, seq_len), 512
    tm_down = min(512, seq_len)
    tq = min(512, seq_len)
    pages_per_step = 4
    win_states = (state_kv_w128, state_kv_w512, state_kv_w2048)
    cache = cache_kv_diff

    xp = x_prompt.reshape(mp, d_model)
    xs = x_sample.reshape(ms, d_model)
    n_a_layers = (depth + 1) // 2
    kv_diff_all, kv_diff_s = None, []
    win_p = [[] for _ in B_GROUPS]
    win_s = [[] for _ in B_GROUPS]
    conv_p, conv_s = [], []

    for i in range(depth):
        hp = rmsnorm_bf16(xp, attn_norm[i], tm)
        hs = rmsnorm_bf16(xs, attn_norm[i], ms)
        if i % 2 == 0:
            a = i // 2
            lam_init = 0.8 - 0.6 * math.exp(-0.3 * i)
            w = a_w_qkv[a]
            qp = matmul_headnorm(hp, w, a_q_gain[a], 0, aw, aw, LANES // 2, _BF16, tm, tn)
            kv_diff_all = matmul_headnorm(hp, w, a_k_gain[a], aw, 2 * aw, aw, LANES // 2, _F32, tm, tn,
                                          stack=(kv_diff_all, n_a_layers, a))
            qs = matmul_headnorm(hs, w, a_q_gain[a], 0, aw, aw, LANES // 2, _BF16, ms, tn)
            kvs = matmul_headnorm(hs, w, a_k_gain[a], aw, 2 * aw, aw, LANES // 2, _F32, ms, tn)
            op = diff_attn_prompt(qp, kv_diff_all, a, a_lambda[a], a_sub_gain[a], lam_init, bsz,
                                  seq_len, a_heads, tq)
            os_ = diff_attn_sample(qs, kvs, cache, a, page_table, a_lambda[a], a_sub_gain[a],
                                   lam_init, a_heads, steps, pages_per_step)
            xp = matmul_residual(op, a_w_o[a], xp, tm, tn)
            xs = matmul_residual(os_, a_w_o[a], xs, ms, tn)
            kv_diff_s.append(kvs.reshape(db, steps, 2, a_heads, LANES))
        else:
            b = i // 2
            w = b_w_qkv[b]
            qp = matmul_headnorm(hp, w, b_q_gain[b], 0, bw, bw, LANES, _F32, tm, tn)
            kvp = matmul_headnorm(hp, w, b_k_gain[b], bw, 2 * bw, bw, LANES, _F32, tm, tn)
            qs = matmul_headnorm(hs, w, b_q_gain[b], 0, bw, bw, LANES, _F32, ms, tn)
            kvs = matmul_headnorm(hs, w, b_k_gain[b], bw, 2 * bw, bw, LANES, _F32, ms, tn)
            op = dilated_attn_prompt(qp, kvp, bsz, seq_len, B_GROUPS, hg)
            os_ = dilated_attn_sample(qs, kvs, win_states, b, B_GROUPS, hg, db, steps)
            xp = matmul_residual(op, b_w_o[b], xp, tm, tn)
            xs = matmul_residual(os_, b_w_o[b], xs, ms, tn)
            kp4 = kvp.reshape(bsz, seq_len, 2, ng, hg, LANES)
            ks4 = kvs.reshape(db, steps, 2, ng, hg, LANES)
            for g, (window, _) in enumerate(B_GROUPS):
                keep = min(window, seq_len)
                win_p[g].append(kp4[:, seq_len - keep:, :, g])
                win_s[g].append(jnp.concatenate([win_states[g][b][:, steps:], ks4[:, :, :, g]], axis=1))
        hp = rmsnorm_bf16(xp, ffn_norm[i], tm)
        hs = rmsnorm_bf16(xs, ffn_norm[i], ms)
        actp, tail = ffn_gate_up_prompt(hp, ffn_w_gate[i], ffn_w_up[i], ffn_conv_w[i], ffn_conv_b[i],
                                        seq_len, tm, tn)
        acts, gs = ffn_gate_up_sample(hs, ffn_w_gate[i], ffn_w_up[i], ffn_conv_w[i], ffn_conv_b[i],
                                      state_conv[i], tn)
        xp = matmul_residual(actp, ffn_w_down[i], xp, tm_down, tn)
        xs = matmul_residual(acts, ffn_w_down[i], xs, ms, tn)
        conv_p.append(tail[:, 6:])
        conv_s.append(gs.reshape(db, steps, d_ff)[:, steps - 2:])

    return (xp.reshape(bsz, seq_len, d_model), xs.reshape(db, steps, d_model),
            kv_diff_all.reshape(n_a_layers, bsz, seq_len, 2, a_heads, LANES), jnp.stack(kv_diff_s),
            jnp.stack(win_p[0]), jnp.stack(win_s[0]), jnp.stack(win_p[1]), jnp.stack(win_s[1]),
            jnp.stack(win_p[2]), jnp.stack(win_s[2]), jnp.stack(conv_p), jnp.stack(conv_s))
```

```python
import functools
import math

import jax
import jax.numpy as jnp
from jax import lax
from jax.experimental import pallas as pl
from jax.experimental.pallas import tpu as pltpu

EPS = 1e-6
NEG_INF = -1e30
LANES = 128
SUBLANES = 8
MXU_COLS = 256
B_GROUPS = ((128, 1), (512, 4), (2048, 16))
VMEM_LIMIT = 56 * 1024 * 1024

_BF16 = jnp.bfloat16
_F32 = jnp.float32
_NT = (((1,), (1,)), ((), ()))


def _params(*sem):
    return pltpu.CompilerParams(dimension_semantics=sem, vmem_limit_bytes=VMEM_LIMIT)


def _alibi_slopes(n):
    return 2.0 ** (-8.0 * jnp.arange(1, n + 1, dtype=_F32) / n)


def _rmsnorm_kernel(x_ref, g_ref, o_ref):
    x = x_ref[...]
    inv = lax.rsqrt(jnp.mean(x * x, axis=-1, keepdims=True) + EPS)
    o_ref[...] = (x * inv * g_ref[...]).astype(o_ref.dtype)


def rmsnorm_bf16(x, gain, tm):
    m, d = x.shape
    return pl.pallas_call(
        _rmsnorm_kernel,
        grid=(pl.cdiv(m, tm),),
        in_specs=[pl.BlockSpec((tm, d), lambda i: (i, 0)),
                  pl.BlockSpec((1, d), lambda i: (0, 0))],
        out_specs=pl.BlockSpec((tm, d), lambda i: (i, 0)),
        out_shape=jax.ShapeDtypeStruct((m, d), _BF16),
        compiler_params=_params("parallel"),
        name="rmsnorm",
    )(x, gain.reshape(1, d))


def _cast_weight(w_ref, wb_ref):
    @pl.when(pl.program_id(1) == 0)
    def _():
        wb_ref[...] = w_ref[...].astype(_BF16)


def _col_chunks(tn):
    step = min(tn, MXU_COLS)
    return [slice(c, c + step) for c in range(0, tn, step)]


def _mm_res_kernel(a_ref, w_ref, r_ref, o_ref, wb_ref):
    _cast_weight(w_ref, wb_ref)
    for sl in _col_chunks(o_ref.shape[1]):
        acc = jnp.dot(a_ref[...], wb_ref[:, sl], preferred_element_type=_F32)
        o_ref[:, sl] = r_ref[:, sl] + acc


def matmul_residual(a, w, res, tm, tn):
    m, k = a.shape
    n = w.shape[1]
    return pl.pallas_call(
        _mm_res_kernel,
        grid=(n // tn, pl.cdiv(m, tm)),
        in_specs=[pl.BlockSpec((tm, k), lambda j, i: (i, 0)),
                  pl.BlockSpec((k, tn), lambda j, i: (0, j)),
                  pl.BlockSpec((tm, tn), lambda j, i: (i, j))],
        out_specs=pl.BlockSpec((tm, tn), lambda j, i: (i, j)),
        out_shape=jax.ShapeDtypeStruct((m, n), _F32),
        scratch_shapes=[pltpu.VMEM((k, tn), _BF16)],
        compiler_params=_params("arbitrary", "arbitrary"),
        name="matmul_residual",
    )(a, w, res)


def _head_rms(c, gain, group):
    sq = c * c
    if group == LANES:
        ms = jnp.sum(sq, axis=-1, keepdims=True) * (1.0 / group)
    else:
        lane = lax.broadcasted_iota(jnp.int32, c.shape, 1)
        low = lane < group
        s_lo = jnp.sum(jnp.where(low, sq, 0.0), axis=-1, keepdims=True)
        s_hi = jnp.sum(jnp.where(low, 0.0, sq), axis=-1, keepdims=True)
        ms = jnp.where(low, s_lo, s_hi) * (1.0 / group)
    return c * lax.rsqrt(ms + EPS) * gain


def _mm_norm_kernel(a_ref, w_ref, g_ref, *rest, group, norm_blocks, n_blocks):
    o_ref, wb_ref = rest[-2:]
    _cast_weight(w_ref, wb_ref)
    chunks = _col_chunks(o_ref.shape[1])

    def normed():
        g = g_ref[...]
        for sl in chunks:
            acc = jnp.dot(a_ref[...], wb_ref[:, sl], preferred_element_type=_F32)
            for c in range(sl.start, sl.stop, LANES):
                o_ref[:, c:c + LANES] = _head_rms(acc[:, c - sl.start:c - sl.start + LANES],
                                                  g, group).astype(o_ref.dtype)

    def raw():
        for sl in chunks:
            acc = jnp.dot(a_ref[...], wb_ref[:, sl], preferred_element_type=_F32)
            o_ref[:, sl] = acc.astype(o_ref.dtype)

    if norm_blocks >= n_blocks:
        normed()
    elif norm_blocks == 0:
        raw()
    else:
        pl.when(pl.program_id(0) < norm_blocks)(normed)
        pl.when(pl.program_id(0) >= norm_blocks)(raw)


def matmul_headnorm(a, w, gain, col0, ncols, norm_cols, group, out_dtype, tm, tn, stack=None):
    m, k = a.shape
    assert col0 % tn == 0 and ncols % tn == 0 and norm_cols % tn == 0
    assert LANES % group == 0 and tn % LANES == 0
    off = col0 // tn
    n_blocks = ncols // tn
    g = jnp.tile(gain.astype(_F32), LANES // group).reshape(1, LANES)
    in_specs = [pl.BlockSpec((tm, k), lambda j, i: (i, 0)),
                pl.BlockSpec((k, tn), lambda j, i: (0, j + off)),
                pl.BlockSpec((1, LANES), lambda j, i: (0, 0))]
    args = [a, w, g]
    aliases = {}
    if stack is None:
        out_spec = pl.BlockSpec((tm, tn), lambda j, i: (i, j))
        out_shape = jax.ShapeDtypeStruct((m, ncols), out_dtype)
    else:
        buf, li = stack
        assert buf.shape[1:] == (m, ncols) and buf.dtype == out_dtype
        out_spec = pl.BlockSpec((None, tm, tn), lambda j, i: (li, i, j))
        out_shape = jax.ShapeDtypeStruct(buf.shape, out_dtype)
        in_specs.append(pl.BlockSpec(memory_space=pl.ANY))
        args.append(buf)
        aliases = {3: 0}
    return pl.pallas_call(
        functools.partial(_mm_norm_kernel, group=group, norm_blocks=norm_cols // tn,
                          n_blocks=n_blocks),
        grid=(n_blocks, pl.cdiv(m, tm)),
        in_specs=in_specs,
        out_specs=out_spec,
        out_shape=out_shape,
        input_output_aliases=aliases,
        scratch_shapes=[pltpu.VMEM((k, tn), _BF16)],
        compiler_params=_params("arbitrary", "arbitrary"),
        name="matmul_headnorm",
    )(*args)


def _gated(g, g1, g2, u, cw, cb):
    c = cb + g2 * cw[0:1] + g1 * cw[1:2] + g * cw[2:3]
    return (c * jax.nn.sigmoid(c) * u).astype(_BF16)


def _ffn_prompt_kernel(a_ref, wg_ref, wu_ref, cw_ref, cb_ref, act_ref, tail_ref,
                       wgb_ref, wub_ref, carry_ref, *, tiles_per_seq):
    _cast_weight(wg_ref, wgb_ref)
    _cast_weight(wu_ref, wub_ref)
    tm = a_ref.shape[0]

    @pl.when(pl.program_id(1) % tiles_per_seq == 0)
    def _():
        carry_ref[...] = jnp.zeros_like(carry_ref)

    for sl in _col_chunks(act_ref.shape[1]):
        g = jnp.dot(a_ref[...], wgb_ref[:, sl], preferred_element_type=_F32)
        u = jnp.dot(a_ref[...], wub_ref[:, sl], preferred_element_type=_F32)
        prev = carry_ref[:, sl]
        row = lax.broadcasted_iota(jnp.int32, g.shape, 0)
        g1 = jnp.where(row == 0, prev[7:8], pltpu.roll(g, 1, axis=0))
        g2 = jnp.where(row == 0, prev[6:7], jnp.where(row == 1, prev[7:8], pltpu.roll(g, 2, axis=0)))
        act_ref[:, sl] = _gated(g, g1, g2, u, cw_ref[:, sl], cb_ref[:, sl])
        tail = g[tm - 8:tm]
        carry_ref[:, sl] = tail
        tail_ref[:, sl] = tail


def ffn_gate_up_prompt(a, w_gate, w_up, conv_w, conv_b, seq_len, tm, tn):
    m, k = a.shape
    f = w_gate.shape[1]
    assert seq_len % tm == 0 and m % seq_len == 0 and f % tn == 0
    tps = seq_len // tm
    return pl.pallas_call(
        functools.partial(_ffn_prompt_kernel, tiles_per_seq=tps),
        grid=(f // tn, m // tm),
        in_specs=[pl.BlockSpec((tm, k), lambda j, i: (i, 0)),
                  pl.BlockSpec((k, tn), lambda j, i: (0, j)),
                  pl.BlockSpec((k, tn), lambda j, i: (0, j)),
                  pl.BlockSpec((3, tn), lambda j, i: (0, j)),
                  pl.BlockSpec((1, tn), lambda j, i: (0, j))],
        out_specs=[pl.BlockSpec((tm, tn), lambda j, i: (i, j)),
                   pl.BlockSpec((None, 8, tn), lambda j, i: (i // tps, 0, j))],
        out_shape=[jax.ShapeDtypeStruct((m, f), _BF16),
                   jax.ShapeDtypeStruct((m // seq_len, 8, f), _F32)],
        scratch_shapes=[pltpu.VMEM((k, tn), _BF16), pltpu.VMEM((k, tn), _BF16),
                        pltpu.VMEM((8, tn), _F32)],
        compiler_params=_params("arbitrary", "arbitrary"),
        name="ffn_gate_up_prompt",
    )(a, w_gate, w_up, conv_w, conv_b.reshape(1, f))


def _ffn_sample_kernel(a_ref, wg_ref, wu_ref, cw_ref, cb_ref, s0_ref, s1_ref, act_ref, g_ref,
                       *, steps):
    a = a_ref[...]
    g = jnp.dot(a, wg_ref[...].astype(_BF16), preferred_element_type=_F32)
    u = jnp.dot(a, wu_ref[...].astype(_BF16), preferred_element_type=_F32)
    pos = lax.broadcasted_iota(jnp.int32, g.shape, 0) % steps
    s0 = s0_ref[...]
    s1 = s1_ref[...]
    g1 = jnp.where(pos == 0, s1, pltpu.roll(g, 1, axis=0))
    g2 = jnp.where(pos == 0, s0, jnp.where(pos == 1, s1, pltpu.roll(g, 2, axis=0)))
    act_ref[...] = _gated(g, g1, g2, u, cw_ref[...], cb_ref[...])
    g_ref[...] = g


def ffn_gate_up_sample(a, w_gate, w_up, conv_w, conv_b, state, tn):
    m, k = a.shape
    f = w_gate.shape[1]
    db = state.shape[0]
    steps = m // db
    assert steps >= 2 and f % tn == 0
    s0 = jnp.repeat(state[:, 0], steps, axis=0)
    s1 = jnp.repeat(state[:, 1], steps, axis=0)
    col = lambda j: (0, j)
    return pl.pallas_call(
        functools.partial(_ffn_sample_kernel, steps=steps),
        grid=(f // tn,),
        in_specs=[pl.BlockSpec((m, k), lambda j: (0, 0)),
                  pl.BlockSpec((k, tn), col), pl.BlockSpec((k, tn), col),
                  pl.BlockSpec((3, tn), col), pl.BlockSpec((1, tn), col),
                  pl.BlockSpec((m, tn), col), pl.BlockSpec((m, tn), col)],
        out_specs=[pl.BlockSpec((m, tn), col), pl.BlockSpec((m, tn), col)],
        out_shape=[jax.ShapeDtypeStruct((m, f), _BF16), jax.ShapeDtypeStruct((m, f), _F32)],
        compiler_params=_params("arbitrary"),
        name="ffn_gate_up_sample",
    )(a, w_gate, w_up, conv_w, conv_b.reshape(1, f), s0, s1)


def _diff_lambda(lam_ref, lam_init):
    lv = lam_ref[...]
    a = jnp.sum(lv[0:1] * lv[1:2], axis=-1, keepdims=True)
    b = jnp.sum(lv[2:3] * lv[3:4], axis=-1, keepdims=True)
    return jnp.exp(a) - jnp.exp(b) + lam_init


def _sub_norm(o, sg_ref, lam_init):
    inv = lax.rsqrt(jnp.mean(o * o, axis=-1, keepdims=True) + EPS)
    return o * inv * sg_ref[...] * (1.0 - lam_init)


def _split_sub_heads(q, group_rows, rows_first_map):
    lane = lax.broadcasted_iota(jnp.int32, q.shape, 1)
    row = lax.broadcasted_iota(jnp.int32, q.shape, 0) % group_rows
    keep = (lane < LANES // 2) == (row < rows_first_map)
    return jnp.where(keep, q, jnp.zeros_like(q))


def _diff_prompt_kernel(slopes_ref, lam_ref, q_ref, k_ref, v_ref, sg_ref, o_ref,
                        kb_ref, vt_ref, rb_ref, rbd_ref, m_ref, l_ref, acc_ref,
                        *, tq, lam_init, scale):
    h = pl.program_id(1)
    qi = pl.program_id(2)
    slope = slopes_ref[h]

    @pl.when(qi == 0)
    def _():
        kb_ref[...] = k_ref[...].astype(_BF16)
        for jb in range(vt_ref.shape[0]):
            vt_ref[jb] = v_ref[jb * tq:(jb + 1) * tq, :].T.astype(_BF16)
        rel = (lax.broadcasted_iota(jnp.int32, (tq, tq), 1)
               - lax.broadcasted_iota(jnp.int32, (tq, tq), 0))
        rb = -slope * rel.astype(_F32)
        rb_ref[...] = rb
        rbd_ref[...] = jnp.where(rel >= 0, rb, NEG_INF)

    qs = (q_ref[...].astype(_F32) * scale).astype(_BF16)
    lane = lax.broadcasted_iota(jnp.int32, qs.shape, 1)
    zero = jnp.zeros_like(qs)
    q_maps = (jnp.where(lane < LANES // 2, qs, zero), jnp.where(lane < LANES // 2, zero, qs))
    q0 = qi * tq

    m_ref[...] = jnp.full_like(m_ref, NEG_INF)
    l_ref[...] = jnp.zeros_like(l_ref)
    acc_ref[...] = jnp.zeros_like(acc_ref)

    def step(kb, vt, bias, shift):
        for mp, qm in enumerate(q_maps):
            s = lax.dot_general(kb, qm, _NT, preferred_element_type=_F32) + bias
            m = m_ref[mp]
            m_new = jnp.maximum(m, jnp.max(s, axis=0, keepdims=True) - shift)
            alpha = jnp.exp(m - m_new)
            p = jnp.exp(s - (m_new + shift))
            l_ref[mp] = alpha * l_ref[mp] + jnp.sum(p, axis=0, keepdims=True)
            m_ref[mp] = m_new
            acc_ref[mp] = alpha * acc_ref[mp] + jnp.dot(vt, p.astype(_BF16),
                                                        preferred_element_type=_F32)

    def body(j, carry):
        k0 = pl.multiple_of(j * tq, tq)
        shift = slope * (q0 - k0).astype(_F32)
        step(kb_ref[pl.ds(k0, tq), :], vt_ref[j], rb_ref[...], shift)
        return carry

    lax.fori_loop(0, qi, body, 0)
    kd = pl.multiple_of(q0, tq)
    step(kb_ref[pl.ds(kd, tq), :], vt_ref[qi], rbd_ref[...], 0.0)
    lam = _diff_lambda(lam_ref, lam_init)
    od = acc_ref[0] / l_ref[0] - lam * (acc_ref[1] / l_ref[1])
    inv = lax.rsqrt(jnp.mean(od * od, axis=0, keepdims=True) + EPS)
    o_ref[...] = ((od * inv).T * sg_ref[...] * (1.0 - lam_init)).astype(o_ref.dtype)


def diff_attn_prompt(q, kv, li, lam_vec, sub_gain, lam_init, bsz, seq_len, n_heads, tq):
    nq = seq_len // tq
    slopes = _alibi_slopes(n_heads)
    return pl.pallas_call(
        functools.partial(_diff_prompt_kernel, tq=tq, lam_init=lam_init,
                          scale=(LANES // 2) ** -0.5),
        grid=(bsz, n_heads, nq),
        in_specs=[pl.BlockSpec(memory_space=pltpu.SMEM),
                  pl.BlockSpec(lam_vec.shape, lambda b, h, i: (0, 0)),
                  pl.BlockSpec((tq, LANES), lambda b, h, i: (b * nq + i, h)),
                  pl.BlockSpec((None, seq_len, LANES), lambda b, h, i: (li, b, h)),
                  pl.BlockSpec((None, seq_len, LANES), lambda b, h, i: (li, b, n_heads + h)),
                  pl.BlockSpec((1, LANES), lambda b, h, i: (0, 0))],
        out_specs=pl.BlockSpec((tq, LANES), lambda b, h, i: (b * nq + i, h)),
        out_shape=jax.ShapeDtypeStruct((bsz * seq_len, n_heads * LANES), _BF16),
        scratch_shapes=[pltpu.VMEM((seq_len, LANES), _BF16), pltpu.VMEM((nq, LANES, tq), _BF16),
                        pltpu.VMEM((tq, tq), _F32), pltpu.VMEM((tq, tq), _F32),
                        pltpu.VMEM((2, 1, tq), _F32), pltpu.VMEM((2, 1, tq), _F32),
                        pltpu.VMEM((2, LANES, tq), _F32)],
        compiler_params=_params("arbitrary", "arbitrary", "arbitrary"),
        name="diff_attn_prompt",
    )(slopes, lam_vec, q, kv, kv, sub_gain.reshape(1, LANES))


def _heads_per_tile(n_heads):
    return SUBLANES if n_heads % SUBLANES == 0 else n_heads


def _diff_sample_kernel(pt_ref, q_ref, bias_ref, biasn_ref, slope_ref, lam_ref, sg_ref, new_ref,
                        *rest, n_heads, steps, page, n_pages, pages_per_step, lam_init, scale):
    del pt_ref
    page_refs = rest[:pages_per_step]
    o_ref, qs_ref, m_ref, l_ref, acc_ref = rest[pages_per_step:]
    p = pl.program_id(1)
    rows = 2 * steps * n_heads

    @pl.when(p == 0)
    def _():
        q = _split_sub_heads(q_ref[...], 2 * steps, steps)
        qs_ref[...] = (q.astype(_F32) * scale).astype(_BF16)
        m_ref[...] = jnp.full_like(m_ref, NEG_INF)
        l_ref[...] = jnp.zeros_like(l_ref)
        acc_ref[...] = jnp.zeros_like(acc_ref)

    ht = _heads_per_tile(n_heads)
    rows_t = 2 * steps * ht

    def attend(kv_refs, shifts, b_ref):
        nk = kv_refs[0].shape[0] * ht
        for t in range(n_heads // ht):
            rs = slice(t * rows_t, (t + 1) * rows_t)
            hs = slice(t * ht, (t + 1) * ht)
            sh = [x if isinstance(x, float) else x[rs] for x in shifts]
            s = [lax.dot_general(qs_ref[rs, :], r[:, 0, hs, :].reshape(nk, LANES).astype(_BF16), _NT,
                                 preferred_element_type=_F32) + b_ref[rs, :] for r in kv_refs]
            m = m_ref[rs, :]
            m_new = functools.reduce(
                jnp.maximum, [jnp.max(si, axis=-1, keepdims=True) - shi for si, shi in zip(s, sh)], m)
            alpha = jnp.exp(m - m_new)
            e = [jnp.exp(si - (m_new + shi)) for si, shi in zip(s, sh)]
            l_ref[rs, :] = alpha * l_ref[rs, :] + sum(jnp.sum(ei, axis=-1, keepdims=True) for ei in e)
            m_ref[rs, :] = m_new
            pv = sum(jnp.dot(ei.astype(_BF16), r[:, 1, hs, :].reshape(nk, LANES).astype(_BF16),
                             preferred_element_type=_F32) for ei, r in zip(e, kv_refs))
            acc_ref[rs, :] = alpha * acc_ref[rs, :] + pv

    slope = slope_ref[...]
    firsts = [(n_pages - (p * pages_per_step + i)) * page for i in range(pages_per_step)]
    attend(page_refs, [slope * f.astype(_F32) for f in firsts], bias_ref)

    @pl.when(p == n_pages // pages_per_step - 1)
    def _():
        attend([new_ref], [0.0], biasn_ref)
        o = acc_ref[...] / l_ref[...]
        lam = _diff_lambda(lam_ref, lam_init)
        od = o - lam * pltpu.roll(o, rows - steps, axis=0)
        o_ref[...] = _sub_norm(od, sg_ref, lam_init)


def diff_attn_sample(q, kv_new, cache, layer, page_table, lam_vec, sub_gain, lam_init,
                     n_heads, steps, pages_per_step):
    db, n_pages = page_table.shape
    page = cache.shape[2]
    hw = n_heads * LANES
    rows = 2 * steps * n_heads
    new_keys = 16
    assert rows % 8 == 0 and n_pages % pages_per_step == 0 and steps <= new_keys
    qr = q.reshape(db, steps, n_heads, 1, LANES).transpose(0, 2, 3, 1, 4)
    qr = jnp.broadcast_to(qr, (db, n_heads, 2, steps, LANES)).reshape(db, rows, LANES)
    new = jnp.pad(kv_new.reshape(db, steps, 2, n_heads, LANES),
                  ((0, 0), (0, new_keys - steps), (0, 0), (0, 0), (0, 0)))
    slope_rows = jnp.repeat(_alibi_slopes(n_heads), 2 * steps).reshape(rows, 1)

    ht = _heads_per_tile(n_heads)

    def bias_table(n_keys, causal):
        r = jnp.arange(rows)[:, None]
        c = jnp.arange(n_keys * ht)[None, :]
        back = r % steps - c // ht
        ok = (r // (2 * steps)) % ht == c % ht
        if causal:
            ok = ok & (back >= 0)
        return jnp.where(ok, -slope_rows * back.astype(_F32), NEG_INF)

    page_spec = lambda i: pl.BlockSpec(
        (None, None, page, 2, n_heads, LANES),
        lambda b, p, pt: (layer, pt[b * n_pages + p * pages_per_step + i], 0, 0, 0, 0))
    const = lambda shape: pl.BlockSpec(shape, lambda b, p, pt: (0,) * len(shape))
    grid_spec = pltpu.PrefetchScalarGridSpec(
        num_scalar_prefetch=1,
        grid=(db, n_pages // pages_per_step),
        in_specs=[pl.BlockSpec((None, rows, LANES), lambda b, p, pt: (b, 0, 0)),
                  const((rows, page * ht)), const((rows, new_keys * ht)),
                  const((rows, 1)), const(lam_vec.shape), const((1, LANES)),
                  pl.BlockSpec((None, new_keys, 2, n_heads, LANES), lambda b, p, pt: (b, 0, 0, 0, 0))]
                 + [page_spec(i) for i in range(pages_per_step)],
        out_specs=pl.BlockSpec((None, rows, LANES), lambda b, p, pt: (b, 0, 0)),
        scratch_shapes=[pltpu.VMEM((rows, LANES), _BF16), pltpu.VMEM((rows, 1), _F32),
                        pltpu.VMEM((rows, 1), _F32), pltpu.VMEM((rows, LANES), _F32)])
    o = pl.pallas_call(
        functools.partial(_diff_sample_kernel, n_heads=n_heads, steps=steps, page=page,
                          n_pages=n_pages, pages_per_step=pages_per_step, lam_init=lam_init,
                          scale=(LANES // 2) ** -0.5),
        grid_spec=grid_spec,
        out_shape=jax.ShapeDtypeStruct((db, rows, LANES), _F32),
        compiler_params=_params("arbitrary", "arbitrary"),
        name="diff_attn_sample",
    )(page_table.reshape(-1), qr, bias_table(page, False), bias_table(new_keys, True), slope_rows,
      lam_vec, sub_gain.reshape(1, LANES), new, *([cache] * pages_per_step))
    o = o.reshape(db, n_heads, 2 * steps, LANES)[:, :, :steps]
    return o.transpose(0, 2, 1, 3).reshape(db * steps, hw).astype(_BF16)


def _dilated_prompt_kernel(slopes_ref, *refs, groups, seq_len, heads_per_group, scale):
    ng = len(groups)
    q_refs, k_refs, v_refs = refs[:ng], refs[ng:2 * ng], refs[2 * ng:3 * ng]
    o_ref, og_ref, lse_ref = refs[3 * ng:]
    j = pl.program_id(1)

    for g, (window, dil) in enumerate(groups):
        blk = window // dil
        sub_len = seq_len // dil
        assert sub_len % blk == 0
        nb = sub_len // blk
        slope = slopes_ref[g * heads_per_group + j]
        qi = lax.broadcasted_iota(jnp.int32, (blk, 2 * blk), 0)
        ci = lax.broadcasted_iota(jnp.int32, (blk, 2 * blk), 1)
        sub_dist = qi + blk - ci
        in_band = (sub_dist >= 0) & (sub_dist <= blk)
        bias = -slope * (sub_dist * dil).astype(_F32)
        q_ref, k_ref, v_ref = q_refs[g], k_refs[g], v_refs[g]

        def body(it, _):
            r = it // nb
            n = it % nb
            start = n * (blk * dil) + r
            prev = jnp.maximum(start - blk * dil, r)
            rows_c = pl.ds(start, blk, stride=dil) if dil > 1 else pl.ds(start, blk)
            rows_p = pl.ds(prev, blk, stride=dil) if dil > 1 else pl.ds(prev, blk)
            qb = q_ref[rows_c, :].astype(_BF16)
            kk = jnp.concatenate([k_ref[rows_p, :], k_ref[rows_c, :]], axis=0).astype(_BF16)
            vv = jnp.concatenate([v_ref[rows_p, :], v_ref[rows_c, :]], axis=0).astype(_BF16)
            s = lax.dot_general(qb, kk, _NT, preferred_element_type=_F32) * scale
            valid = in_band & (ci + (n - 1) * blk >= 0)
            s = jnp.where(valid, s + bias, NEG_INF)
            m = jnp.max(s, axis=-1, keepdims=True)
            e = jnp.exp(s - m)
            l = jnp.sum(e, axis=-1, keepdims=True)
            o = jnp.dot(e.astype(_BF16), vv, preferred_element_type=_F32) / l
            og_ref[g, rows_c, :] = o
            lse_ref[g, rows_c, :] = jnp.broadcast_to(m + jnp.log(l), (blk, LANES))
            return 0

        lax.fori_loop(0, dil * nb, body, 0, unroll=16)

    chunk = 256
    for c in range(seq_len // chunk):
        sl = pl.ds(c * chunk, chunk)
        lses = [lse_ref[g, sl, :] for g in range(ng)]
        mx = functools.reduce(jnp.maximum, lses)
        ws = [jnp.exp(x - mx) for x in lses]
        num = functools.reduce(lambda a, b: a + b, [w * og_ref[g, sl, :] for g, w in enumerate(ws)])
        den = functools.reduce(lambda a, b: a + b, ws)
        o_ref[sl, :] = (num / den).astype(o_ref.dtype)


def dilated_attn_prompt(q, kv, bsz, seq_len, groups, heads_per_group):
    ng = len(groups)
    nh = ng * heads_per_group
    slopes = _alibi_slopes(nh)
    blk_spec = lambda off: pl.BlockSpec((seq_len, LANES), lambda b, j, off=off: (b, off + j))
    in_specs = [pl.BlockSpec(memory_space=pltpu.SMEM)]
    in_specs += [blk_spec(g * heads_per_group) for g in range(ng)]
    in_specs += [blk_spec(g * heads_per_group) for g in range(ng)]
    in_specs += [blk_spec(nh + g * heads_per_group) for g in range(ng)]
    return pl.pallas_call(
        functools.partial(_dilated_prompt_kernel, groups=groups, seq_len=seq_len,
                          heads_per_group=heads_per_group, scale=LANES ** -0.5),
        grid=(bsz, heads_per_group),
        in_specs=in_specs,
        out_specs=pl.BlockSpec((seq_len, LANES), lambda b, j: (b, j)),
        out_shape=jax.ShapeDtypeStruct((bsz * seq_len, heads_per_group * LANES), _BF16),
        scratch_shapes=[pltpu.VMEM((ng, seq_len, LANES), _F32), pltpu.VMEM((ng, seq_len, LANES), _F32)],
        compiler_params=_params("arbitrary", "arbitrary"),
        name="dilated_attn_prompt",
    )(slopes, *([q] * ng), *([kv] * ng), *([kv] * ng))


def _dilated_sample_kernel(q_ref, new_ref, *refs, groups, steps, scale):
    ng = len(groups)
    st_refs, bp_refs, bn_refs = refs[:ng], refs[ng:2 * ng], refs[2 * ng:3 * ng]
    o_ref = refs[3 * ng]
    for t in range(steps):
        ms, ls, accs = [], [], []
        for g, (_, dil) in enumerate(groups):
            sub = t if dil > 1 else 0
            qb = q_ref[t, g].astype(_BF16)
            flat = lambda x: x.reshape(x.shape[0] * x.shape[1], LANES).astype(_BF16)
            kp, vp = flat(st_refs[g][:, sub, 0]), flat(st_refs[g][:, sub, 1])
            kn, vn = flat(new_ref[:, 0, g]), flat(new_ref[:, 1, g])
            sp = lax.dot_general(qb, kp, _NT, preferred_element_type=_F32) * scale + bp_refs[g][t]
            sn = lax.dot_general(qb, kn, _NT, preferred_element_type=_F32) * scale + bn_refs[g][t]
            m = jnp.maximum(jnp.max(sp, axis=-1, keepdims=True), jnp.max(sn, axis=-1, keepdims=True))
            ep = jnp.exp(sp - m)
            en = jnp.exp(sn - m)
            ls.append(jnp.sum(ep, axis=-1, keepdims=True) + jnp.sum(en, axis=-1, keepdims=True))
            accs.append(jnp.dot(ep.astype(_BF16), vp, preferred_element_type=_F32)
                        + jnp.dot(en.astype(_BF16), vn, preferred_element_type=_F32))
            ms.append(m)
        mx = functools.reduce(jnp.maximum, ms)
        ws = [jnp.exp(m - mx) for m in ms]
        num = functools.reduce(lambda a, b: a + b, [w * o for w, o in zip(ws, accs)])
        den = functools.reduce(lambda a, b: a + b, [w * l for w, l in zip(ws, ls)])
        o_ref[t] = num / den


def dilated_attn_sample(q, kv_new, states, layer, groups, heads_per_group, db, steps):
    ng = len(groups)
    hg = heads_per_group
    new_keys = 16
    assert steps <= new_keys and all(steps <= dil or dil == 1 for _, dil in groups)
    q5 = q.reshape(db, steps, ng, hg, LANES)
    new = jnp.pad(kv_new.reshape(db, steps, 2, ng, hg, LANES),
                  ((0, 0), (0, new_keys - steps)) + ((0, 0),) * 4)
    slopes = _alibi_slopes(ng * hg).reshape(ng, hg, 1)
    head = jnp.arange(hg)[:, None]
    t = jnp.arange(steps)[:, None, None]
    views, st_specs, bias_p, bias_n = [], [], [], []
    for g, (window, dil) in enumerate(groups):
        st = states[g]
        assert st.shape[2] == window
        nb = window // dil
        views.append(st.reshape(st.shape[0], db, nb, dil, 2, hg, LANES))
        st_specs.append(pl.BlockSpec((None, None, nb, min(dil, steps), 2, hg, LANES),
                                     lambda b: (layer, b, 0, 0, 0, 0, 0)))
        col = jnp.arange(nb * hg)[None, :]
        m, hp = col // hg, col % hg
        coln = jnp.arange(new_keys * hg)[None, :]
        tn, hn = coln // hg, coln % hg
        if dil == 1:
            back_p, ok_p = nb + t - m, (hp == head) & (m >= t)
            back_n, ok_n = t - tn, (hn == head) & (tn <= t)
        else:
            back_p, ok_p = nb - m + 0 * t, (hp == head) & (t >= 0)
            back_n, ok_n = 0 * (t - tn), (hn == head) & (tn == t)
        bias_p.append(jnp.where(ok_p, -slopes[g] * (back_p * dil).astype(_F32), NEG_INF))
        bias_n.append(jnp.where(ok_n, -slopes[g] * (back_n * dil).astype(_F32), NEG_INF))
    full = lambda x: pl.BlockSpec(x.shape, lambda b: (0,) * x.ndim)
    o = pl.pallas_call(
        functools.partial(_dilated_sample_kernel, groups=groups, steps=steps, scale=LANES ** -0.5),
        grid=(db,),
        in_specs=[pl.BlockSpec((None, steps, ng, hg, LANES), lambda b: (b, 0, 0, 0, 0)),
                  pl.BlockSpec((None, new_keys, 2, ng, hg, LANES), lambda b: (b, 0, 0, 0, 0, 0))]
                 + st_specs + [full(x) for x in bias_p] + [full(x) for x in bias_n],
        out_specs=pl.BlockSpec((None, steps, hg, LANES), lambda b: (b, 0, 0, 0)),
        out_shape=jax.ShapeDtypeStruct((db, steps, hg, LANES), _F32),
        compiler_params=_params("arbitrary"),
        name="dilated_attn_sample",
    )(q5, new, *views, *bias_p, *bias_n)
    return o.reshape(db * steps, hg * LANES).astype(_BF16)


def _shift_state_kernel(state_ref, new_ref, out_ref, sems, *, steps):
    layers, db, window = state_ref.shape[:3]
    keep = window - steps
    copies = []
    for l in range(layers):
        for b in range(db):
            i = 2 * (l * db + b)
            copies.append(pltpu.make_async_copy(state_ref.at[l, b, pl.ds(steps, keep)],
                                                out_ref.at[l, b, pl.ds(0, keep)], sems.at[i]))
            copies.append(pltpu.make_async_copy(new_ref.at[l, b],
                                                out_ref.at[l, b, pl.ds(keep, steps)], sems.at[i + 1]))
    for c in copies:
        c.start()
    for c in copies:
        c.wait()


def shift_window_state(state, new):
    layers, db = state.shape[:2]
    steps = new.shape[2]
    return pl.pallas_call(
        functools.partial(_shift_state_kernel, steps=steps),
        in_specs=[pl.BlockSpec(memory_space=pl.ANY), pl.BlockSpec(memory_space=pl.ANY)],
        out_specs=pl.BlockSpec(memory_space=pl.ANY),
        out_shape=jax.ShapeDtypeStruct(state.shape, state.dtype),
        scratch_shapes=[pltpu.SemaphoreType.DMA((2 * layers * db,))],
        name="shift_window_state",
    )(state, new)


def kernel(x_prompt, x_sample, cache_kv_diff, state_kv_w128, state_kv_w512, state_kv_w2048,
           state_conv, page_table, attn_norm, ffn_norm, a_w_qkv, a_w_o, a_q_gain, a_k_gain,
           a_lambda, a_sub_gain, b_w_qkv, b_w_o, b_q_gain, b_k_gain, ffn_w_gate, ffn_w_up,
           ffn_w_down, ffn_conv_w, ffn_conv_b):
    bsz, seq_len, d_model = x_prompt.shape
    db, steps, _ = x_sample.shape
    depth = attn_norm.shape[0]
    d_ff = ffn_w_gate.shape[2]
    a_heads = a_w_qkv.shape[2] // (3 * LANES)
    ng = len(B_GROUPS)
    b_heads = b_w_qkv.shape[2] // (3 * LANES)
    hg = b_heads // ng
    aw = a_heads * LANES
    bw = b_heads * LANES
    mp, ms = bsz * seq_len, db * steps
    tm, tn = min(1024, seq_len), 512
    tm_down = min(512, seq_len)
    tq = min(512, seq_len)
    pages_per_step = 4
    win_states = (state_kv_w128, state_kv_w512, state_kv_w2048)
    cache = cache_kv_diff

    xp = x_prompt.reshape(mp, d_model)
    xs = x_sample.reshape(ms, d_model)
    n_a_layers = (depth + 1) // 2
    kv_diff_all, kv_diff_s = jnp.zeros((n_a_layers, mp, 2 * aw), _F32), []
    win_p = [[] for _ in B_GROUPS]
    win_s = [[] for _ in B_GROUPS]
    conv_p, conv_s = [], []

    for i in range(depth):
        hp = rmsnorm_bf16(xp, attn_norm[i], tm)
        hs = rmsnorm_bf16(xs, attn_norm[i], ms)
        if i % 2 == 0:
            a = i // 2
            lam_init = 0.8 - 0.6 * math.exp(-0.3 * i)
            w = a_w_qkv[a]
            qp = matmul_headnorm(hp, w, a_q_gain[a], 0, aw, aw, LANES // 2, _BF16, tm, tn)
            kv_diff_all = matmul_headnorm(hp, w, a_k_gain[a], aw, 2 * aw, aw, LANES // 2, _F32, tm, tn,
                                          stack=(kv_diff_all, a))
            qs = matmul_headnorm(hs, w, a_q_gain[a], 0, aw, aw, LANES // 2, _BF16, ms, tn)
            kvs = matmul_headnorm(hs, w, a_k_gain[a], aw, 2 * aw, aw, LANES // 2, _F32, ms, tn)
            op = diff_attn_prompt(qp, kv_diff_all, a, a_lambda[a], a_sub_gain[a], lam_init, bsz,
                                  seq_len, a_heads, tq)
            os_ = diff_attn_sample(qs, kvs, cache, a, page_table, a_lambda[a], a_sub_gain[a],
                                   lam_init, a_heads, steps, pages_per_step)
            xp = matmul_residual(op, a_w_o[a], xp, tm, tn)
            xs = matmul_residual(os_, a_w_o[a], xs, ms, tn)
            kv_diff_s.append(kvs.reshape(db, steps, 2, a_heads, LANES))
        else:
            b = i // 2
            w = b_w_qkv[b]
            qp = matmul_headnorm(hp, w, b_q_gain[b], 0, bw, bw, LANES, _F32, tm, tn)
            kvp = matmul_headnorm(hp, w, b_k_gain[b], bw, 2 * bw, bw, LANES, _F32, tm, tn)
            qs = matmul_headnorm(hs, w, b_q_gain[b], 0, bw, bw, LANES, _F32, ms, tn)
            kvs = matmul_headnorm(hs, w, b_k_gain[b], bw, 2 * bw, bw, LANES, _F32, ms, tn)
            op = dilated_attn_prompt(qp, kvp, bsz, seq_len, B_GROUPS, hg)
            os_ = dilated_attn_sample(qs, kvs, win_states, b, B_GROUPS, hg, db, steps)
            xp = matmul_residual(op, b_w_o[b], xp, tm, tn)
            xs = matmul_residual(os_, b_w_o[b], xs, ms, tn)
            kp4 = kvp.reshape(bsz, seq_len, 2, ng, hg, LANES)
            ks4 = kvs.reshape(db, steps, 2, ng, hg, LANES)
            for g, (window, _) in enumerate(B_GROUPS):
                keep = min(window, seq_len)
                win_p[g].append(kp4[:, seq_len - keep:, :, g])
                win_s[g].append(ks4[:, :, :, g])
        hp = rmsnorm_bf16(xp, ffn_norm[i], tm)
        hs = rmsnorm_bf16(xs, ffn_norm[i], ms)
        actp, tail = ffn_gate_up_prompt(hp, ffn_w_gate[i], ffn_w_up[i], ffn_conv_w[i], ffn_conv_b[i],
                                        seq_len, tm, tn)
        acts, gs = ffn_gate_up_sample(hs, ffn_w_gate[i], ffn_w_up[i], ffn_conv_w[i], ffn_conv_b[i],
                                      state_conv[i], tn)
        xp = matmul_residual(actp, ffn_w_down[i], xp, tm_down, tn)
        xs = matmul_residual(acts, ffn_w_down[i], xs, ms, tn)
        conv_p.append(tail[:, 6:])
        conv_s.append(gs.reshape(db, steps, d_ff)[:, steps - 2:])

    new_state = [shift_window_state(win_states[g], jnp.stack(win_s[g])) for g in range(ng)]
    return (xp.reshape(bsz, seq_len, d_model), xs.reshape(db, steps, d_model),
            kv_diff_all.reshape(n_a_layers, bsz, seq_len, 2, a_heads, LANES), jnp.stack(kv_diff_s),
            jnp.stack(win_p[0]), new_state[0], jnp.stack(win_p[1]), new_state[1],
            jnp.stack(win_p[2]), new_state[2], jnp.stack(conv_p), jnp.stack(conv_s))
```

```python
import functools
import math

import jax
import jax.numpy as jnp
from jax import lax
from jax.experimental import pallas as pl
from jax.experimental.pallas import tpu as pltpu

EPS = 1e-6
NEG_INF = -1e30
LANES = 128
SUBLANES = 8
MXU_COLS = 256
B_GROUPS = ((128, 1), (512, 4), (2048, 16))
VMEM_LIMIT = 56 * 1024 * 1024

_BF16 = jnp.bfloat16
_F32 = jnp.float32
_NT = (((1,), (1,)), ((), ()))


def _params(*sem):
    return pltpu.CompilerParams(dimension_semantics=sem, vmem_limit_bytes=VMEM_LIMIT)


def _alibi_slopes(n):
    return 2.0 ** (-8.0 * jnp.arange(1, n + 1, dtype=_F32) / n)


def _rmsnorm_kernel(x_ref, g_ref, o_ref):
    x = x_ref[...]
    inv = lax.rsqrt(jnp.mean(x * x, axis=-1, keepdims=True) + EPS)
    o_ref[...] = (x * inv * g_ref[...]).astype(o_ref.dtype)


def rmsnorm_bf16(x, gain, tm):
    m, d = x.shape
    return pl.pallas_call(
        _rmsnorm_kernel,
        grid=(pl.cdiv(m, tm),),
        in_specs=[pl.BlockSpec((tm, d), lambda i: (i, 0)),
                  pl.BlockSpec((1, d), lambda i: (0, 0))],
        out_specs=pl.BlockSpec((tm, d), lambda i: (i, 0)),
        out_shape=jax.ShapeDtypeStruct((m, d), _BF16),
        compiler_params=_params("parallel"),
        name="rmsnorm",
    )(x, gain.reshape(1, d))


def _cast_weight(w_ref, wb_ref):
    @pl.when(pl.program_id(1) == 0)
    def _():
        wb_ref[...] = w_ref[...].astype(_BF16)


def _col_chunks(tn):
    step = min(tn, MXU_COLS)
    return [slice(c, c + step) for c in range(0, tn, step)]


def _mm_res_kernel(a_ref, w_ref, r_ref, o_ref, wb_ref):
    _cast_weight(w_ref, wb_ref)
    for sl in _col_chunks(o_ref.shape[1]):
        acc = jnp.dot(a_ref[...], wb_ref[:, sl], preferred_element_type=_F32)
        o_ref[:, sl] = r_ref[:, sl] + acc


def matmul_residual(a, w, li, res, tm, tn):
    m, k = a.shape
    n = w.shape[2]
    return pl.pallas_call(
        _mm_res_kernel,
        grid=(n // tn, pl.cdiv(m, tm)),
        in_specs=[pl.BlockSpec((tm, k), lambda j, i: (i, 0)),
                  pl.BlockSpec((None, k, tn), lambda j, i: (li, 0, j)),
                  pl.BlockSpec((tm, tn), lambda j, i: (i, j))],
        out_specs=pl.BlockSpec((tm, tn), lambda j, i: (i, j)),
        out_shape=jax.ShapeDtypeStruct((m, n), _F32),
        scratch_shapes=[pltpu.VMEM((k, tn), _BF16)],
        compiler_params=_params("arbitrary", "arbitrary"),
        name="matmul_residual",
    )(a, w, res)


def _head_rms(c, gain, group):
    sq = c * c
    if group == LANES:
        ms = jnp.sum(sq, axis=-1, keepdims=True) * (1.0 / group)
    else:
        lane = lax.broadcasted_iota(jnp.int32, c.shape, 1)
        low = lane < group
        s_lo = jnp.sum(jnp.where(low, sq, 0.0), axis=-1, keepdims=True)
        s_hi = jnp.sum(jnp.where(low, 0.0, sq), axis=-1, keepdims=True)
        ms = jnp.where(low, s_lo, s_hi) * (1.0 / group)
    return c * lax.rsqrt(ms + EPS) * gain


def _mm_norm_kernel(a_ref, w_ref, g_ref, *rest, group, norm_blocks, n_blocks):
    o_ref, wb_ref = rest[-2:]
    _cast_weight(w_ref, wb_ref)
    chunks = _col_chunks(o_ref.shape[1])

    def normed():
        g = g_ref[...]
        for sl in chunks:
            acc = jnp.dot(a_ref[...], wb_ref[:, sl], preferred_element_type=_F32)
            for c in range(sl.start, sl.stop, LANES):
                o_ref[:, c:c + LANES] = _head_rms(acc[:, c - sl.start:c - sl.start + LANES],
                                                  g, group).astype(o_ref.dtype)

    def raw():
        for sl in chunks:
            acc = jnp.dot(a_ref[...], wb_ref[:, sl], preferred_element_type=_F32)
            o_ref[:, sl] = acc.astype(o_ref.dtype)

    if norm_blocks >= n_blocks:
        normed()
    elif norm_blocks == 0:
        raw()
    else:
        pl.when(pl.program_id(0) < norm_blocks)(normed)
        pl.when(pl.program_id(0) >= norm_blocks)(raw)


def matmul_headnorm(a, w, li, gain, col0, ncols, norm_cols, group, out_dtype, tm, tn, stack=None):
    m, k = a.shape
    assert col0 % tn == 0 and ncols % tn == 0 and norm_cols % tn == 0
    assert LANES % group == 0 and tn % LANES == 0
    off = col0 // tn
    n_blocks = ncols // tn
    g = jnp.tile(gain.astype(_F32), LANES // group).reshape(1, LANES)
    in_specs = [pl.BlockSpec((tm, k), lambda j, i: (i, 0)),
                pl.BlockSpec((None, k, tn), lambda j, i: (li, 0, j + off)),
                pl.BlockSpec((1, LANES), lambda j, i: (0, 0))]
    args = [a, w, g]
    aliases = {}
    if stack is None:
        out_spec = pl.BlockSpec((tm, tn), lambda j, i: (i, j))
        out_shape = jax.ShapeDtypeStruct((m, ncols), out_dtype)
    else:
        buf, slot = stack
        assert buf.shape[1:] == (m, ncols) and buf.dtype == out_dtype
        out_spec = pl.BlockSpec((None, tm, tn), lambda j, i: (slot, i, j))
        out_shape = jax.ShapeDtypeStruct(buf.shape, out_dtype)
        in_specs.append(pl.BlockSpec(memory_space=pl.ANY))
        args.append(buf)
        aliases = {3: 0}
    return pl.pallas_call(
        functools.partial(_mm_norm_kernel, group=group, norm_blocks=norm_cols // tn,
                          n_blocks=n_blocks),
        grid=(n_blocks, pl.cdiv(m, tm)),
        in_specs=in_specs,
        out_specs=out_spec,
        out_shape=out_shape,
        input_output_aliases=aliases,
        scratch_shapes=[pltpu.VMEM((k, tn), _BF16)],
        compiler_params=_params("arbitrary", "arbitrary"),
        name="matmul_headnorm",
    )(*args)


def _gated(g, g1, g2, u, cw, cb):
    c = cb + g2 * cw[0:1] + g1 * cw[1:2] + g * cw[2:3]
    return (c * jax.nn.sigmoid(c) * u).astype(_BF16)


def _ffn_prompt_kernel(a_ref, wg_ref, wu_ref, cw_ref, cb_ref, act_ref, tail_ref,
                       wgb_ref, wub_ref, carry_ref, *, tiles_per_seq):
    _cast_weight(wg_ref, wgb_ref)
    _cast_weight(wu_ref, wub_ref)
    tm = a_ref.shape[0]

    @pl.when(pl.program_id(1) % tiles_per_seq == 0)
    def _():
        carry_ref[...] = jnp.zeros_like(carry_ref)

    for sl in _col_chunks(act_ref.shape[1]):
        g = jnp.dot(a_ref[...], wgb_ref[:, sl], preferred_element_type=_F32)
        u = jnp.dot(a_ref[...], wub_ref[:, sl], preferred_element_type=_F32)
        prev = carry_ref[:, sl]
        row = lax.broadcasted_iota(jnp.int32, g.shape, 0)
        g1 = jnp.where(row == 0, prev[7:8], pltpu.roll(g, 1, axis=0))
        g2 = jnp.where(row == 0, prev[6:7], jnp.where(row == 1, prev[7:8], pltpu.roll(g, 2, axis=0)))
        act_ref[:, sl] = _gated(g, g1, g2, u, cw_ref[:, sl], cb_ref[:, sl])
        tail = g[tm - 8:tm]
        carry_ref[:, sl] = tail
        tail_ref[:, sl] = tail


def ffn_gate_up_prompt(a, w_gate, w_up, li, conv_w, conv_b, seq_len, tm, tn):
    m, k = a.shape
    f = w_gate.shape[2]
    assert seq_len % tm == 0 and m % seq_len == 0 and f % tn == 0
    tps = seq_len // tm
    return pl.pallas_call(
        functools.partial(_ffn_prompt_kernel, tiles_per_seq=tps),
        grid=(f // tn, m // tm),
        in_specs=[pl.BlockSpec((tm, k), lambda j, i: (i, 0)),
                  pl.BlockSpec((None, k, tn), lambda j, i: (li, 0, j)),
                  pl.BlockSpec((None, k, tn), lambda j, i: (li, 0, j)),
                  pl.BlockSpec((3, tn), lambda j, i: (0, j)),
                  pl.BlockSpec((1, tn), lambda j, i: (0, j))],
        out_specs=[pl.BlockSpec((tm, tn), lambda j, i: (i, j)),
                   pl.BlockSpec((None, 8, tn), lambda j, i: (i // tps, 0, j))],
        out_shape=[jax.ShapeDtypeStruct((m, f), _BF16),
                   jax.ShapeDtypeStruct((m // seq_len, 8, f), _F32)],
        scratch_shapes=[pltpu.VMEM((k, tn), _BF16), pltpu.VMEM((k, tn), _BF16),
                        pltpu.VMEM((8, tn), _F32)],
        compiler_params=_params("arbitrary", "arbitrary"),
        name="ffn_gate_up_prompt",
    )(a, w_gate, w_up, conv_w, conv_b.reshape(1, f))


def _ffn_sample_kernel(a_ref, wg_ref, wu_ref, cw_ref, cb_ref, s0_ref, s1_ref, act_ref, g_ref,
                       *, steps):
    a = a_ref[...]
    g = jnp.dot(a, wg_ref[...].astype(_BF16), preferred_element_type=_F32)
    u = jnp.dot(a, wu_ref[...].astype(_BF16), preferred_element_type=_F32)
    pos = lax.broadcasted_iota(jnp.int32, g.shape, 0) % steps
    s0 = s0_ref[...]
    s1 = s1_ref[...]
    g1 = jnp.where(pos == 0, s1, pltpu.roll(g, 1, axis=0))
    g2 = jnp.where(pos == 0, s0, jnp.where(pos == 1, s1, pltpu.roll(g, 2, axis=0)))
    act_ref[...] = _gated(g, g1, g2, u, cw_ref[...], cb_ref[...])
    g_ref[...] = g


def ffn_gate_up_sample(a, w_gate, w_up, li, conv_w, conv_b, state, tn):
    m, k = a.shape
    f = w_gate.shape[2]
    db = state.shape[0]
    steps = m // db
    assert steps >= 2 and f % tn == 0
    s0 = jnp.repeat(state[:, 0], steps, axis=0)
    s1 = jnp.repeat(state[:, 1], steps, axis=0)
    col = lambda j: (0, j)
    return pl.pallas_call(
        functools.partial(_ffn_sample_kernel, steps=steps),
        grid=(f // tn,),
        in_specs=[pl.BlockSpec((m, k), lambda j: (0, 0)),
                  pl.BlockSpec((None, k, tn), lambda j: (li, 0, j)),
                  pl.BlockSpec((None, k, tn), lambda j: (li, 0, j)),
                  pl.BlockSpec((3, tn), col), pl.BlockSpec((1, tn), col),
                  pl.BlockSpec((m, tn), col), pl.BlockSpec((m, tn), col)],
        out_specs=[pl.BlockSpec((m, tn), col), pl.BlockSpec((m, tn), col)],
        out_shape=[jax.ShapeDtypeStruct((m, f), _BF16), jax.ShapeDtypeStruct((m, f), _F32)],
        compiler_params=_params("arbitrary"),
        name="ffn_gate_up_sample",
    )(a, w_gate, w_up, conv_w, conv_b.reshape(1, f), s0, s1)


def _diff_lambda(lam_ref, lam_init):
    lv = lam_ref[...]
    a = jnp.sum(lv[0:1] * lv[1:2], axis=-1, keepdims=True)
    b = jnp.sum(lv[2:3] * lv[3:4], axis=-1, keepdims=True)
    return jnp.exp(a) - jnp.exp(b) + lam_init


def _sub_norm(o, sg_ref, lam_init):
    inv = lax.rsqrt(jnp.mean(o * o, axis=-1, keepdims=True) + EPS)
    return o * inv * sg_ref[...] * (1.0 - lam_init)


def _split_sub_heads(q, group_rows, rows_first_map):
    lane = lax.broadcasted_iota(jnp.int32, q.shape, 1)
    row = lax.broadcasted_iota(jnp.int32, q.shape, 0) % group_rows
    keep = (lane < LANES // 2) == (row < rows_first_map)
    return jnp.where(keep, q, jnp.zeros_like(q))


def _diff_prompt_kernel(slopes_ref, lam_ref, q_ref, k_ref, v_ref, sg_ref, o_ref,
                        kb_ref, vt_ref, rb_ref, rbd_ref, m_ref, l_ref, acc_ref,
                        *, tq, lam_init, scale):
    h = pl.program_id(1)
    qi = pl.program_id(2)
    slope = slopes_ref[h]

    @pl.when(qi == 0)
    def _():
        kb_ref[...] = k_ref[...].astype(_BF16)
        for jb in range(vt_ref.shape[0]):
            vt_ref[jb] = v_ref[jb * tq:(jb + 1) * tq, :].T.astype(_BF16)
        rel = (lax.broadcasted_iota(jnp.int32, (tq, tq), 1)
               - lax.broadcasted_iota(jnp.int32, (tq, tq), 0))
        rb = -slope * rel.astype(_F32)
        rb_ref[...] = rb
        rbd_ref[...] = jnp.where(rel >= 0, rb, NEG_INF)

    qs = (q_ref[...].astype(_F32) * scale).astype(_BF16)
    lane = lax.broadcasted_iota(jnp.int32, qs.shape, 1)
    zero = jnp.zeros_like(qs)
    q_maps = (jnp.where(lane < LANES // 2, qs, zero), jnp.where(lane < LANES // 2, zero, qs))
    q0 = qi * tq

    m_ref[...] = jnp.full_like(m_ref, NEG_INF)
    l_ref[...] = jnp.zeros_like(l_ref)
    acc_ref[...] = jnp.zeros_like(acc_ref)

    def step(kb, vt, bias, shift):
        for mp, qm in enumerate(q_maps):
            s = lax.dot_general(kb, qm, _NT, preferred_element_type=_F32) + bias
            m = m_ref[mp]
            m_new = jnp.maximum(m, jnp.max(s, axis=0, keepdims=True) - shift)
            alpha = jnp.exp(m - m_new)
            p = jnp.exp(s - (m_new + shift))
            l_ref[mp] = alpha * l_ref[mp] + jnp.sum(p, axis=0, keepdims=True)
            m_ref[mp] = m_new
            acc_ref[mp] = alpha * acc_ref[mp] + jnp.dot(vt, p.astype(_BF16),
                                                        preferred_element_type=_F32)

    def body(j, carry):
        k0 = pl.multiple_of(j * tq, tq)
        shift = slope * (q0 - k0).astype(_F32)
        step(kb_ref[pl.ds(k0, tq), :], vt_ref[j], rb_ref[...], shift)
        return carry

    lax.fori_loop(0, qi, body, 0)
    kd = pl.multiple_of(q0, tq)
    step(kb_ref[pl.ds(kd, tq), :], vt_ref[qi], rbd_ref[...], 0.0)
    lam = _diff_lambda(lam_ref, lam_init)
    od = acc_ref[0] / l_ref[0] - lam * (acc_ref[1] / l_ref[1])
    inv = lax.rsqrt(jnp.mean(od * od, axis=0, keepdims=True) + EPS)
    o_ref[...] = ((od * inv).T * sg_ref[...] * (1.0 - lam_init)).astype(o_ref.dtype)


def diff_attn_prompt(q, kv, li, lam_vec, sub_gain, lam_init, bsz, seq_len, n_heads, tq):
    nq = seq_len // tq
    slopes = _alibi_slopes(n_heads)
    return pl.pallas_call(
        functools.partial(_diff_prompt_kernel, tq=tq, lam_init=lam_init,
                          scale=(LANES // 2) ** -0.5),
        grid=(bsz, n_heads, nq),
        in_specs=[pl.BlockSpec(memory_space=pltpu.SMEM),
                  pl.BlockSpec(lam_vec.shape, lambda b, h, i: (0, 0)),
                  pl.BlockSpec((tq, LANES), lambda b, h, i: (b * nq + i, h)),
                  pl.BlockSpec((None, seq_len, LANES), lambda b, h, i: (li, b, h)),
                  pl.BlockSpec((None, seq_len, LANES), lambda b, h, i: (li, b, n_heads + h)),
                  pl.BlockSpec((1, LANES), lambda b, h, i: (0, 0))],
        out_specs=pl.BlockSpec((tq, LANES), lambda b, h, i: (b * nq + i, h)),
        out_shape=jax.ShapeDtypeStruct((bsz * seq_len, n_heads * LANES), _BF16),
        scratch_shapes=[pltpu.VMEM((seq_len, LANES), _BF16), pltpu.VMEM((nq, LANES, tq), _BF16),
                        pltpu.VMEM((tq, tq), _F32), pltpu.VMEM((tq, tq), _F32),
                        pltpu.VMEM((2, 1, tq), _F32), pltpu.VMEM((2, 1, tq), _F32),
                        pltpu.VMEM((2, LANES, tq), _F32)],
        compiler_params=_params("arbitrary", "arbitrary", "arbitrary"),
        name="diff_attn_prompt",
    )(slopes, lam_vec, q, kv, kv, sub_gain.reshape(1, LANES))


def _heads_per_tile(n_heads):
    return SUBLANES if n_heads % SUBLANES == 0 else n_heads


def _diff_sample_kernel(pt_ref, q_ref, bias_ref, biasn_ref, slope_ref, lam_ref, sg_ref, new_ref,
                        *rest, n_heads, steps, page, n_pages, pages_per_step, lam_init, scale):
    del pt_ref
    page_refs = rest[:pages_per_step]
    o_ref, qs_ref, m_ref, l_ref, acc_ref = rest[pages_per_step:]
    p = pl.program_id(1)
    rows = 2 * steps * n_heads

    @pl.when(p == 0)
    def _():
        q = _split_sub_heads(q_ref[...], 2 * steps, steps)
        qs_ref[...] = (q.astype(_F32) * scale).astype(_BF16)
        m_ref[...] = jnp.full_like(m_ref, NEG_INF)
        l_ref[...] = jnp.zeros_like(l_ref)
        acc_ref[...] = jnp.zeros_like(acc_ref)

    ht = _heads_per_tile(n_heads)
    rows_t = 2 * steps * ht

    def attend(kv_refs, shifts, b_ref):
        nk = kv_refs[0].shape[0] * ht
        for t in range(n_heads // ht):
            rs = slice(t * rows_t, (t + 1) * rows_t)
            hs = slice(t * ht, (t + 1) * ht)
            sh = [x if isinstance(x, float) else x[rs] for x in shifts]
            s = [lax.dot_general(qs_ref[rs, :], r[:, 0, hs, :].reshape(nk, LANES).astype(_BF16), _NT,
                                 preferred_element_type=_F32) + b_ref[rs, :] for r in kv_refs]
            m = m_ref[rs, :]
            m_new = functools.reduce(
                jnp.maximum, [jnp.max(si, axis=-1, keepdims=True) - shi for si, shi in zip(s, sh)], m)
            alpha = jnp.exp(m - m_new)
            e = [jnp.exp(si - (m_new + shi)) for si, shi in zip(s, sh)]
            l_ref[rs, :] = alpha * l_ref[rs, :] + sum(jnp.sum(ei, axis=-1, keepdims=True) for ei in e)
            m_ref[rs, :] = m_new
            pv = sum(jnp.dot(ei.astype(_BF16), r[:, 1, hs, :].reshape(nk, LANES).astype(_BF16),
                             preferred_element_type=_F32) for ei, r in zip(e, kv_refs))
            acc_ref[rs, :] = alpha * acc_ref[rs, :] + pv

    slope = slope_ref[...]
    firsts = [(n_pages - (p * pages_per_step + i)) * page for i in range(pages_per_step)]
    attend(page_refs, [slope * f.astype(_F32) for f in firsts], bias_ref)

    @pl.when(p == n_pages // pages_per_step - 1)
    def _():
        attend([new_ref], [0.0], biasn_ref)
        o = acc_ref[...] / l_ref[...]
        lam = _diff_lambda(lam_ref, lam_init)
        od = o - lam * pltpu.roll(o, rows - steps, axis=0)
        o_ref[...] = _sub_norm(od, sg_ref, lam_init)


def diff_attn_sample(q, kv_new, cache, layer, page_table, lam_vec, sub_gain, lam_init,
                     n_heads, steps, pages_per_step):
    db, n_pages = page_table.shape
    page = cache.shape[2]
    hw = n_heads * LANES
    rows = 2 * steps * n_heads
    new_keys = 16
    assert rows % 8 == 0 and n_pages % pages_per_step == 0 and steps <= new_keys
    qr = q.reshape(db, steps, n_heads, 1, LANES).transpose(0, 2, 3, 1, 4)
    qr = jnp.broadcast_to(qr, (db, n_heads, 2, steps, LANES)).reshape(db, rows, LANES)
    new = jnp.pad(kv_new.reshape(db, steps, 2, n_heads, LANES),
                  ((0, 0), (0, new_keys - steps), (0, 0), (0, 0), (0, 0)))
    slope_rows = jnp.repeat(_alibi_slopes(n_heads), 2 * steps).reshape(rows, 1)

    ht = _heads_per_tile(n_heads)

    def bias_table(n_keys, causal):
        r = jnp.arange(rows)[:, None]
        c = jnp.arange(n_keys * ht)[None, :]
        back = r % steps - c // ht
        ok = (r // (2 * steps)) % ht == c % ht
        if causal:
            ok = ok & (back >= 0)
        return jnp.where(ok, -slope_rows * back.astype(_F32), NEG_INF)

    page_spec = lambda i: pl.BlockSpec(
        (None, None, page, 2, n_heads, LANES),
        lambda b, p, pt: (layer, pt[b * n_pages + p * pages_per_step + i], 0, 0, 0, 0))
    const = lambda shape: pl.BlockSpec(shape, lambda b, p, pt: (0,) * len(shape))
    grid_spec = pltpu.PrefetchScalarGridSpec(
        num_scalar_prefetch=1,
        grid=(db, n_pages // pages_per_step),
        in_specs=[pl.BlockSpec((None, rows, LANES), lambda b, p, pt: (b, 0, 0)),
                  const((rows, page * ht)), const((rows, new_keys * ht)),
                  const((rows, 1)), const(lam_vec.shape), const((1, LANES)),
                  pl.BlockSpec((None, new_keys, 2, n_heads, LANES), lambda b, p, pt: (b, 0, 0, 0, 0))]
                 + [page_spec(i) for i in range(pages_per_step)],
        out_specs=pl.BlockSpec((None, rows, LANES), lambda b, p, pt: (b, 0, 0)),
        scratch_shapes=[pltpu.VMEM((rows, LANES), _BF16), pltpu.VMEM((rows, 1), _F32),
                        pltpu.VMEM((rows, 1), _F32), pltpu.VMEM((rows, LANES), _F32)])
    o = pl.pallas_call(
        functools.partial(_diff_sample_kernel, n_heads=n_heads, steps=steps, page=page,
                          n_pages=n_pages, pages_per_step=pages_per_step, lam_init=lam_init,
                          scale=(LANES // 2) ** -0.5),
        grid_spec=grid_spec,
        out_shape=jax.ShapeDtypeStruct((db, rows, LANES), _F32),
        compiler_params=_params("arbitrary", "arbitrary"),
        name="diff_attn_sample",
    )(page_table.reshape(-1), qr, bias_table(page, False), bias_table(new_keys, True), slope_rows,
      lam_vec, sub_gain.reshape(1, LANES), new, *([cache] * pages_per_step))
    o = o.reshape(db, n_heads, 2 * steps, LANES)[:, :, :steps]
    return o.transpose(0, 2, 1, 3).reshape(db * steps, hw).astype(_BF16)


def _dilated_prompt_kernel(slopes_ref, *refs, groups, seq_len, heads_per_group, scale):
    ng = len(groups)
    q_refs, k_refs, v_refs = refs[:ng], refs[ng:2 * ng], refs[2 * ng:3 * ng]
    o_ref, og_ref, lse_ref = refs[3 * ng:]
    j = pl.program_id(1)

    for g, (window, dil) in enumerate(groups):
        blk = window // dil
        sub_len = seq_len // dil
        assert sub_len % blk == 0
        nb = sub_len // blk
        slope = slopes_ref[g * heads_per_group + j]
        qi = lax.broadcasted_iota(jnp.int32, (blk, 2 * blk), 0)
        ci = lax.broadcasted_iota(jnp.int32, (blk, 2 * blk), 1)
        sub_dist = qi + blk - ci
        in_band = (sub_dist >= 0) & (sub_dist <= blk)
        bias = -slope * (sub_dist * dil).astype(_F32)
        q_ref, k_ref, v_ref = q_refs[g], k_refs[g], v_refs[g]

        def body(it, _):
            r = it // nb
            n = it % nb
            start = n * (blk * dil) + r
            prev = jnp.maximum(start - blk * dil, r)
            rows_c = pl.ds(start, blk, stride=dil) if dil > 1 else pl.ds(start, blk)
            rows_p = pl.ds(prev, blk, stride=dil) if dil > 1 else pl.ds(prev, blk)
            qb = q_ref[rows_c, :].astype(_BF16)
            kk = jnp.concatenate([k_ref[rows_p, :], k_ref[rows_c, :]], axis=0).astype(_BF16)
            vv = jnp.concatenate([v_ref[rows_p, :], v_ref[rows_c, :]], axis=0).astype(_BF16)
            s = lax.dot_general(qb, kk, _NT, preferred_element_type=_F32) * scale
            valid = in_band & (ci + (n - 1) * blk >= 0)
            s = jnp.where(valid, s + bias, NEG_INF)
            m = jnp.max(s, axis=-1, keepdims=True)
            e = jnp.exp(s - m)
            l = jnp.sum(e, axis=-1, keepdims=True)
            o = jnp.dot(e.astype(_BF16), vv, preferred_element_type=_F32) / l
            og_ref[g, rows_c, :] = o
            lse_ref[g, rows_c, :] = jnp.broadcast_to(m + jnp.log(l), (blk, LANES))
            return 0

        lax.fori_loop(0, dil * nb, body, 0, unroll=16)

    chunk = 256
    for c in range(seq_len // chunk):
        sl = pl.ds(c * chunk, chunk)
        lses = [lse_ref[g, sl, :] for g in range(ng)]
        mx = functools.reduce(jnp.maximum, lses)
        ws = [jnp.exp(x - mx) for x in lses]
        num = functools.reduce(lambda a, b: a + b, [w * og_ref[g, sl, :] for g, w in enumerate(ws)])
        den = functools.reduce(lambda a, b: a + b, ws)
        o_ref[sl, :] = (num / den).astype(o_ref.dtype)


def dilated_attn_prompt(q, kv, bsz, seq_len, groups, heads_per_group):
    ng = len(groups)
    nh = ng * heads_per_group
    slopes = _alibi_slopes(nh)
    blk_spec = lambda off: pl.BlockSpec((seq_len, LANES), lambda b, j, off=off: (b, off + j))
    in_specs = [pl.BlockSpec(memory_space=pltpu.SMEM)]
    in_specs += [blk_spec(g * heads_per_group) for g in range(ng)]
    in_specs += [blk_spec(g * heads_per_group) for g in range(ng)]
    in_specs += [blk_spec(nh + g * heads_per_group) for g in range(ng)]
    return pl.pallas_call(
        functools.partial(_dilated_prompt_kernel, groups=groups, seq_len=seq_len,
                          heads_per_group=heads_per_group, scale=LANES ** -0.5),
        grid=(bsz, heads_per_group),
        in_specs=in_specs,
        out_specs=pl.BlockSpec((seq_len, LANES), lambda b, j: (b, j)),
        out_shape=jax.ShapeDtypeStruct((bsz * seq_len, heads_per_group * LANES), _BF16),
        scratch_shapes=[pltpu.VMEM((ng, seq_len, LANES), _F32), pltpu.VMEM((ng, seq_len, LANES), _F32)],
        compiler_params=_params("arbitrary", "arbitrary"),
        name="dilated_attn_prompt",
    )(slopes, *([q] * ng), *([kv] * ng), *([kv] * ng))


def _dilated_sample_kernel(q_ref, new_ref, *refs, groups, steps, scale):
    ng = len(groups)
    st_refs, bp_refs, bn_refs = refs[:ng], refs[ng:2 * ng], refs[2 * ng:3 * ng]
    o_ref = refs[3 * ng]
    for t in range(steps):
        ms, ls, accs = [], [], []
        for g, (_, dil) in enumerate(groups):
            sub = t if dil > 1 else 0
            qb = q_ref[t, g].astype(_BF16)
            flat = lambda x: x.reshape(x.shape[0] * x.shape[1], LANES).astype(_BF16)
            kp, vp = flat(st_refs[g][:, sub, 0]), flat(st_refs[g][:, sub, 1])
            kn, vn = flat(new_ref[:, 0, g]), flat(new_ref[:, 1, g])
            sp = lax.dot_general(qb, kp, _NT, preferred_element_type=_F32) * scale + bp_refs[g][t]
            sn = lax.dot_general(qb, kn, _NT, preferred_element_type=_F32) * scale + bn_refs[g][t]
            m = jnp.maximum(jnp.max(sp, axis=-1, keepdims=True), jnp.max(sn, axis=-1, keepdims=True))
            ep = jnp.exp(sp - m)
            en = jnp.exp(sn - m)
            ls.append(jnp.sum(ep, axis=-1, keepdims=True) + jnp.sum(en, axis=-1, keepdims=True))
            accs.append(jnp.dot(ep.astype(_BF16), vp, preferred_element_type=_F32)
                        + jnp.dot(en.astype(_BF16), vn, preferred_element_type=_F32))
            ms.append(m)
        mx = functools.reduce(jnp.maximum, ms)
        ws = [jnp.exp(m - mx) for m in ms]
        num = functools.reduce(lambda a, b: a + b, [w * o for w, o in zip(ws, accs)])
        den = functools.reduce(lambda a, b: a + b, [w * l for w, l in zip(ws, ls)])
        o_ref[t] = num / den


def dilated_attn_sample(q, kv_new, states, layer, groups, heads_per_group, db, steps):
    ng = len(groups)
    hg = heads_per_group
    new_keys = 16
    assert steps <= new_keys and all(steps <= dil or dil == 1 for _, dil in groups)
    q5 = q.reshape(db, steps, ng, hg, LANES)
    new = jnp.pad(kv_new.reshape(db, steps, 2, ng, hg, LANES),
                  ((0, 0), (0, new_keys - steps)) + ((0, 0),) * 4)
    slopes = _alibi_slopes(ng * hg).reshape(ng, hg, 1)
    head = jnp.arange(hg)[:, None]
    t = jnp.arange(steps)[:, None, None]
    views, st_specs, bias_p, bias_n = [], [], [], []
    for g, (window, dil) in enumerate(groups):
        st = states[g]
        assert st.shape[2] == window
        nb = window // dil
        views.append(st.reshape(st.shape[0], db, nb, dil, 2, hg, LANES))
        st_specs.append(pl.BlockSpec((None, None, nb, min(dil, steps), 2, hg, LANES),
                                     lambda b: (layer, b, 0, 0, 0, 0, 0)))
        col = jnp.arange(nb * hg)[None, :]
        m, hp = col // hg, col % hg
        coln = jnp.arange(new_keys * hg)[None, :]
        tn, hn = coln // hg, coln % hg
        if dil == 1:
            back_p, ok_p = nb + t - m, (hp == head) & (m >= t)
            back_n, ok_n = t - tn, (hn == head) & (tn <= t)
        else:
            back_p, ok_p = nb - m + 0 * t, (hp == head) & (t >= 0)
            back_n, ok_n = 0 * (t - tn), (hn == head) & (tn == t)
        bias_p.append(jnp.where(ok_p, -slopes[g] * (back_p * dil).astype(_F32), NEG_INF))
        bias_n.append(jnp.where(ok_n, -slopes[g] * (back_n * dil).astype(_F32), NEG_INF))
    full = lambda x: pl.BlockSpec(x.shape, lambda b: (0,) * x.ndim)
    o = pl.pallas_call(
        functools.partial(_dilated_sample_kernel, groups=groups, steps=steps, scale=LANES ** -0.5),
        grid=(db,),
        in_specs=[pl.BlockSpec((None, steps, ng, hg, LANES), lambda b: (b, 0, 0, 0, 0)),
                  pl.BlockSpec((None, new_keys, 2, ng, hg, LANES), lambda b: (b, 0, 0, 0, 0, 0))]
                 + st_specs + [full(x) for x in bias_p] + [full(x) for x in bias_n],
        out_specs=pl.BlockSpec((None, steps, hg, LANES), lambda b: (b, 0, 0, 0)),
        out_shape=jax.ShapeDtypeStruct((db, steps, hg, LANES), _F32),
        compiler_params=_params("arbitrary"),
        name="dilated_attn_sample",
    )(q5, new, *views, *bias_p, *bias_n)
    return o.reshape(db * steps, hg * LANES).astype(_BF16)


def _shift_state_kernel(cur_ref, nxt_ref, new_ref, out_ref, *, steps):
    rows = cur_ref.shape[0]
    out_ref[0:rows - steps] = cur_ref[steps:rows]
    last = pl.program_id(2) == pl.num_programs(2) - 1

    @pl.when(last)
    def _():
        out_ref[rows - steps:rows] = new_ref[...]

    @pl.when(jnp.logical_not(last))
    def _():
        out_ref[rows - steps:rows] = nxt_ref[...]


def shift_window_state(state, new):
    layers, db, window = state.shape[:3]
    steps = new.shape[2]
    rows = min(window, 512)
    assert window % rows == 0 and rows % steps == 0 and rows > steps
    chunks = window // rows
    per = rows // steps
    tile = state.shape[3:]
    nil = (0,) * len(tile)
    return pl.pallas_call(
        functools.partial(_shift_state_kernel, steps=steps),
        grid=(layers, db, chunks),
        in_specs=[pl.BlockSpec((None, None, rows) + tile, lambda l, b, c: (l, b, c) + nil),
                  pl.BlockSpec((None, None, steps) + tile,
                               lambda l, b, c: (l, b, jnp.minimum(c + 1, chunks - 1) * per) + nil),
                  pl.BlockSpec((None, None, steps) + tile, lambda l, b, c: (l, b, 0) + nil)],
        out_specs=pl.BlockSpec((None, None, rows) + tile, lambda l, b, c: (l, b, c) + nil),
        out_shape=jax.ShapeDtypeStruct(state.shape, state.dtype),
        compiler_params=_params("arbitrary", "arbitrary", "arbitrary"),
        name="shift_window_state",
    )(state, state, new)


def kernel(x_prompt, x_sample, cache_kv_diff, state_kv_w128, state_kv_w512, state_kv_w2048,
           state_conv, page_table, attn_norm, ffn_norm, a_w_qkv, a_w_o, a_q_gain, a_k_gain,
           a_lambda, a_sub_gain, b_w_qkv, b_w_o, b_q_gain, b_k_gain, ffn_w_gate, ffn_w_up,
           ffn_w_down, ffn_conv_w, ffn_conv_b):
    bsz, seq_len, d_model = x_prompt.shape
    db, steps, _ = x_sample.shape
    depth = attn_norm.shape[0]
    d_ff = ffn_w_gate.shape[2]
    a_heads = a_w_qkv.shape[2] // (3 * LANES)
    ng = len(B_GROUPS)
    b_heads = b_w_qkv.shape[2] // (3 * LANES)
    hg = b_heads // ng
    aw = a_heads * LANES
    bw = b_heads * LANES
    mp, ms = bsz * seq_len, db * steps
    tm, tn = min(1024, seq_len), 512
    tm_down = min(512, seq_len)
    tq = min(512, seq_len)
    pages_per_step = 4
    win_states = (state_kv_w128, state_kv_w512, state_kv_w2048)
    cache = cache_kv_diff

    xp = x_prompt.reshape(mp, d_model)
    xs = x_sample.reshape(ms, d_model)
    n_a_layers = (depth + 1) // 2
    kv_diff_all, kv_diff_s = jnp.zeros((n_a_layers, mp, 2 * aw), _F32), []
    win_p = [[] for _ in B_GROUPS]
    win_s = [[] for _ in B_GROUPS]
    conv_p, conv_s = [], []

    for i in range(depth):
        hp = rmsnorm_bf16(xp, attn_norm[i], tm)
        hs = rmsnorm_bf16(xs, attn_norm[i], ms)
        if i % 2 == 0:
            a = i // 2
            lam_init = 0.8 - 0.6 * math.exp(-0.3 * i)
            w = a_w_qkv
            qp = matmul_headnorm(hp, w, a, a_q_gain[a], 0, aw, aw, LANES // 2, _BF16, tm, tn)
            kv_diff_all = matmul_headnorm(hp, w, a, a_k_gain[a], aw, 2 * aw, aw, LANES // 2, _F32,
                                          tm, tn, stack=(kv_diff_all, a))
            qs = matmul_headnorm(hs, w, a, a_q_gain[a], 0, aw, aw, LANES // 2, _BF16, ms, tn)
            kvs = matmul_headnorm(hs, w, a, a_k_gain[a], aw, 2 * aw, aw, LANES // 2, _F32, ms, tn)
            op = diff_attn_prompt(qp, kv_diff_all, a, a_lambda[a], a_sub_gain[a], lam_init, bsz,
                                  seq_len, a_heads, tq)
            os_ = diff_attn_sample(qs, kvs, cache, a, page_table, a_lambda[a], a_sub_gain[a],
                                   lam_init, a_heads, steps, pages_per_step)
            xp = matmul_residual(op, a_w_o, a, xp, tm, tn)
            xs = matmul_residual(os_, a_w_o, a, xs, ms, tn)
            kv_diff_s.append(kvs.reshape(db, steps, 2, a_heads, LANES))
        else:
            b = i // 2
            w = b_w_qkv
            qp = matmul_headnorm(hp, w, b, b_q_gain[b], 0, bw, bw, LANES, _F32, tm, tn)
            kvp = matmul_headnorm(hp, w, b, b_k_gain[b], bw, 2 * bw, bw, LANES, _F32, tm, tn)
            qs = matmul_headnorm(hs, w, b, b_q_gain[b], 0, bw, bw, LANES, _F32, ms, tn)
            kvs = matmul_headnorm(hs, w, b, b_k_gain[b], bw, 2 * bw, bw, LANES, _F32, ms, tn)
            op = dilated_attn_prompt(qp, kvp, bsz, seq_len, B_GROUPS, hg)
            os_ = dilated_attn_sample(qs, kvs, win_states, b, B_GROUPS, hg, db, steps)
            xp = matmul_residual(op, b_w_o, b, xp, tm, tn)
            xs = matmul_residual(os_, b_w_o, b, xs, ms, tn)
            kp4 = kvp.reshape(bsz, seq_len, 2, ng, hg, LANES)
            ks4 = kvs.reshape(db, steps, 2, ng, hg, LANES)
            for g, (window, _) in enumerate(B_GROUPS):
                keep = min(window, seq_len)
                win_p[g].append(kp4[:, seq_len - keep:, :, g])
                win_s[g].append(ks4[:, :, :, g])
        hp = rmsnorm_bf16(xp, ffn_norm[i], tm)
        hs = rmsnorm_bf16(xs, ffn_norm[i], ms)
        actp, tail = ffn_gate_up_prompt(hp, ffn_w_gate, ffn_w_up, i, ffn_conv_w[i], ffn_conv_b[i],
                                        seq_len, tm, tn)
        acts, gs = ffn_gate_up_sample(hs, ffn_w_gate, ffn_w_up, i, ffn_conv_w[i], ffn_conv_b[i],
                                      state_conv[i], tn)
        xp = matmul_residual(actp, ffn_w_down, i, xp, tm_down, tn)
        xs = matmul_residual(acts, ffn_w_down, i, xs, ms, tn)
        conv_p.append(tail[:, 6:])
        conv_s.append(gs.reshape(db, steps, d_ff)[:, steps - 2:])

    new_state = [shift_window_state(win_states[g], jnp.stack(win_s[g])) for g in range(ng)]
    return (xp.reshape(bsz, seq_len, d_model), xs.reshape(db, steps, d_model),
            kv_diff_all.reshape(n_a_layers, bsz, seq_len, 2, a_heads, LANES), jnp.stack(kv_diff_s),
            jnp.stack(win_p[0]), new_state[0], jnp.stack(win_p[1]), new_state[1],
            jnp.stack(win_p[2]), new_state[2], jnp.stack(conv_p), jnp.stack(conv_s))
```

```python
import functools
import math

import jax
import jax.numpy as jnp
from jax import lax
from jax.experimental import pallas as pl
from jax.experimental.pallas import tpu as pltpu

EPS = 1e-6
NEG_INF = -1e30
LANES = 128
SUBLANES = 8
MXU_COLS = 256
B_GROUPS = ((128, 1), (512, 4), (2048, 16))
VMEM_LIMIT = 56 * 1024 * 1024

_BF16 = jnp.bfloat16
_F32 = jnp.float32
_NT = (((1,), (1,)), ((), ()))


def _params(*sem):
    return pltpu.CompilerParams(dimension_semantics=sem, vmem_limit_bytes=VMEM_LIMIT)


def _alibi_slopes(n):
    return 2.0 ** (-8.0 * jnp.arange(1, n + 1, dtype=_F32) / n)


def _rmsnorm_kernel(x_ref, g_ref, o_ref):
    x = x_ref[...]
    inv = lax.rsqrt(jnp.mean(x * x, axis=-1, keepdims=True) + EPS)
    o_ref[...] = (x * inv * g_ref[...]).astype(o_ref.dtype)


def rmsnorm_bf16(x, gain, tm):
    m, d = x.shape
    return pl.pallas_call(
        _rmsnorm_kernel,
        grid=(pl.cdiv(m, tm),),
        in_specs=[pl.BlockSpec((tm, d), lambda i: (i, 0)),
                  pl.BlockSpec((1, d), lambda i: (0, 0))],
        out_specs=pl.BlockSpec((tm, d), lambda i: (i, 0)),
        out_shape=jax.ShapeDtypeStruct((m, d), _BF16),
        compiler_params=_params("parallel"),
        name="rmsnorm",
    )(x, gain.reshape(1, d))


def _cast_weight(w_ref, wb_ref):
    @pl.when(pl.program_id(1) == 0)
    def _():
        wb_ref[...] = w_ref[...].astype(_BF16)


def _col_chunks(tn):
    step = min(tn, MXU_COLS)
    return [slice(c, c + step) for c in range(0, tn, step)]


def _mm_res_kernel(a_ref, w_ref, r_ref, *rest):
    o_ref, wb_ref = rest[-2:]
    _cast_weight(w_ref, wb_ref)
    for sl in _col_chunks(o_ref.shape[1]):
        acc = jnp.dot(a_ref[...], wb_ref[:, sl], preferred_element_type=_F32)
        o_ref[:, sl] = r_ref[:, sl] + acc


def matmul_residual(a, w, li, res, tm, tn, after=None):
    m, k = a.shape
    n = w.shape[2]
    in_specs = [pl.BlockSpec((tm, k), lambda j, i: (i, 0)),
                pl.BlockSpec((None, k, tn), lambda j, i: (li, 0, j)),
                pl.BlockSpec((tm, tn), lambda j, i: (i, j))]
    args = [a, w, res]
    if after is not None:
        in_specs.append(pl.BlockSpec(memory_space=pl.ANY))
        args.append(after)
    return pl.pallas_call(
        _mm_res_kernel,
        grid=(n // tn, pl.cdiv(m, tm)),
        in_specs=in_specs,
        out_specs=pl.BlockSpec((tm, tn), lambda j, i: (i, j)),
        out_shape=jax.ShapeDtypeStruct((m, n), _F32),
        scratch_shapes=[pltpu.VMEM((k, tn), _BF16)],
        compiler_params=_params("arbitrary", "arbitrary"),
        name="matmul_residual",
    )(*args)


def _head_rms(c, gain, group):
    sq = c * c
    if group == LANES:
        ms = jnp.sum(sq, axis=-1, keepdims=True) * (1.0 / group)
    else:
        lane = lax.broadcasted_iota(jnp.int32, c.shape, 1)
        low = lane < group
        s_lo = jnp.sum(jnp.where(low, sq, 0.0), axis=-1, keepdims=True)
        s_hi = jnp.sum(jnp.where(low, 0.0, sq), axis=-1, keepdims=True)
        ms = jnp.where(low, s_lo, s_hi) * (1.0 / group)
    return c * lax.rsqrt(ms + EPS) * gain


def _mm_norm_kernel(a_ref, w_ref, g_ref, *rest, group, norm_blocks, n_blocks):
    o_ref, wb_ref = rest[-2:]
    _cast_weight(w_ref, wb_ref)
    chunks = _col_chunks(o_ref.shape[1])

    def normed():
        g = g_ref[...]
        for sl in chunks:
            acc = jnp.dot(a_ref[...], wb_ref[:, sl], preferred_element_type=_F32)
            for c in range(sl.start, sl.stop, LANES):
                o_ref[:, c:c + LANES] = _head_rms(acc[:, c - sl.start:c - sl.start + LANES],
                                                  g, group).astype(o_ref.dtype)

    def raw():
        for sl in chunks:
            acc = jnp.dot(a_ref[...], wb_ref[:, sl], preferred_element_type=_F32)
            o_ref[:, sl] = acc.astype(o_ref.dtype)

    if norm_blocks >= n_blocks:
        normed()
    elif norm_blocks == 0:
        raw()
    else:
        pl.when(pl.program_id(0) < norm_blocks)(normed)
        pl.when(pl.program_id(0) >= norm_blocks)(raw)


def matmul_headnorm(a, w, li, gain, col0, ncols, norm_cols, group, out_dtype, tm, tn, stack=None):
    m, k = a.shape
    assert col0 % tn == 0 and ncols % tn == 0 and norm_cols % tn == 0
    assert LANES % group == 0 and tn % LANES == 0
    off = col0 // tn
    n_blocks = ncols // tn
    g = jnp.tile(gain.astype(_F32), LANES // group).reshape(1, LANES)
    in_specs = [pl.BlockSpec((tm, k), lambda j, i: (i, 0)),
                pl.BlockSpec((None, k, tn), lambda j, i: (li, 0, j + off)),
                pl.BlockSpec((1, LANES), lambda j, i: (0, 0))]
    args = [a, w, g]
    aliases = {}
    if stack is None:
        out_spec = pl.BlockSpec((tm, tn), lambda j, i: (i, j))
        out_shape = jax.ShapeDtypeStruct((m, ncols), out_dtype)
    else:
        buf, slot = stack
        assert buf.shape[1:] == (m, ncols) and buf.dtype == out_dtype
        out_spec = pl.BlockSpec((None, tm, tn), lambda j, i: (slot, i, j))
        out_shape = jax.ShapeDtypeStruct(buf.shape, out_dtype)
        in_specs.append(pl.BlockSpec(memory_space=pl.ANY))
        args.append(buf)
        aliases = {3: 0}
    return pl.pallas_call(
        functools.partial(_mm_norm_kernel, group=group, norm_blocks=norm_cols // tn,
                          n_blocks=n_blocks),
        grid=(n_blocks, pl.cdiv(m, tm)),
        in_specs=in_specs,
        out_specs=out_spec,
        out_shape=out_shape,
        input_output_aliases=aliases,
        scratch_shapes=[pltpu.VMEM((k, tn), _BF16)],
        compiler_params=_params("arbitrary", "arbitrary"),
        name="matmul_headnorm",
    )(*args)


def _gated(g, g1, g2, u, cw, cb):
    c = cb + g2 * cw[0:1] + g1 * cw[1:2] + g * cw[2:3]
    return (c * jax.nn.sigmoid(c) * u).astype(_BF16)


def _ffn_prompt_kernel(a_ref, wg_ref, wu_ref, cw_ref, cb_ref, act_ref, tail_ref,
                       wgb_ref, wub_ref, carry_ref, *, tiles_per_seq):
    _cast_weight(wg_ref, wgb_ref)
    _cast_weight(wu_ref, wub_ref)
    tm = a_ref.shape[0]

    @pl.when(pl.program_id(1) % tiles_per_seq == 0)
    def _():
        carry_ref[...] = jnp.zeros_like(carry_ref)

    for sl in _col_chunks(act_ref.shape[1]):
        g = jnp.dot(a_ref[...], wgb_ref[:, sl], preferred_element_type=_F32)
        u = jnp.dot(a_ref[...], wub_ref[:, sl], preferred_element_type=_F32)
        prev = carry_ref[:, sl]
        row = lax.broadcasted_iota(jnp.int32, g.shape, 0)
        g1 = jnp.where(row == 0, prev[7:8], pltpu.roll(g, 1, axis=0))
        g2 = jnp.where(row == 0, prev[6:7], jnp.where(row == 1, prev[7:8], pltpu.roll(g, 2, axis=0)))
        act_ref[:, sl] = _gated(g, g1, g2, u, cw_ref[:, sl], cb_ref[:, sl])
        tail = g[tm - 8:tm]
        carry_ref[:, sl] = tail
        tail_ref[:, sl] = tail


def ffn_gate_up_prompt(a, w_gate, w_up, li, conv_w, conv_b, seq_len, tm, tn):
    m, k = a.shape
    f = w_gate.shape[2]
    assert seq_len % tm == 0 and m % seq_len == 0 and f % tn == 0
    tps = seq_len // tm
    return pl.pallas_call(
        functools.partial(_ffn_prompt_kernel, tiles_per_seq=tps),
        grid=(f // tn, m // tm),
        in_specs=[pl.BlockSpec((tm, k), lambda j, i: (i, 0)),
                  pl.BlockSpec((None, k, tn), lambda j, i: (li, 0, j)),
                  pl.BlockSpec((None, k, tn), lambda j, i: (li, 0, j)),
                  pl.BlockSpec((3, tn), lambda j, i: (0, j)),
                  pl.BlockSpec((1, tn), lambda j, i: (0, j))],
        out_specs=[pl.BlockSpec((tm, tn), lambda j, i: (i, j)),
                   pl.BlockSpec((None, 8, tn), lambda j, i: (i // tps, 0, j))],
        out_shape=[jax.ShapeDtypeStruct((m, f), _BF16),
                   jax.ShapeDtypeStruct((m // seq_len, 8, f), _F32)],
        scratch_shapes=[pltpu.VMEM((k, tn), _BF16), pltpu.VMEM((k, tn), _BF16),
                        pltpu.VMEM((8, tn), _F32)],
        compiler_params=_params("arbitrary", "arbitrary"),
        name="ffn_gate_up_prompt",
    )(a, w_gate, w_up, conv_w, conv_b.reshape(1, f))


def _ffn_sample_kernel(a_ref, wg_ref, wu_ref, cw_ref, cb_ref, s0_ref, s1_ref, act_ref, g_ref,
                       *, steps):
    a = a_ref[...]
    g = jnp.dot(a, wg_ref[...].astype(_BF16), preferred_element_type=_F32)
    u = jnp.dot(a, wu_ref[...].astype(_BF16), preferred_element_type=_F32)
    pos = lax.broadcasted_iota(jnp.int32, g.shape, 0) % steps
    s0 = s0_ref[...]
    s1 = s1_ref[...]
    g1 = jnp.where(pos == 0, s1, pltpu.roll(g, 1, axis=0))
    g2 = jnp.where(pos == 0, s0, jnp.where(pos == 1, s1, pltpu.roll(g, 2, axis=0)))
    act_ref[...] = _gated(g, g1, g2, u, cw_ref[...], cb_ref[...])
    g_ref[...] = g


def ffn_gate_up_sample(a, w_gate, w_up, li, conv_w, conv_b, state, tn):
    m, k = a.shape
    f = w_gate.shape[2]
    db = state.shape[0]
    steps = m // db
    assert steps >= 2 and f % tn == 0
    s0 = jnp.repeat(state[:, 0], steps, axis=0)
    s1 = jnp.repeat(state[:, 1], steps, axis=0)
    col = lambda j: (0, j)
    return pl.pallas_call(
        functools.partial(_ffn_sample_kernel, steps=steps),
        grid=(f // tn,),
        in_specs=[pl.BlockSpec((m, k), lambda j: (0, 0)),
                  pl.BlockSpec((None, k, tn), lambda j: (li, 0, j)),
                  pl.BlockSpec((None, k, tn), lambda j: (li, 0, j)),
                  pl.BlockSpec((3, tn), col), pl.BlockSpec((1, tn), col),
                  pl.BlockSpec((m, tn), col), pl.BlockSpec((m, tn), col)],
        out_specs=[pl.BlockSpec((m, tn), col), pl.BlockSpec((m, tn), col)],
        out_shape=[jax.ShapeDtypeStruct((m, f), _BF16), jax.ShapeDtypeStruct((m, f), _F32)],
        compiler_params=_params("arbitrary"),
        name="ffn_gate_up_sample",
    )(a, w_gate, w_up, conv_w, conv_b.reshape(1, f), s0, s1)


def _diff_lambda(lam_ref, lam_init):
    lv = lam_ref[...]
    a = jnp.sum(lv[0:1] * lv[1:2], axis=-1, keepdims=True)
    b = jnp.sum(lv[2:3] * lv[3:4], axis=-1, keepdims=True)
    return jnp.exp(a) - jnp.exp(b) + lam_init


def _sub_norm(o, sg_ref, lam_init):
    inv = lax.rsqrt(jnp.mean(o * o, axis=-1, keepdims=True) + EPS)
    return o * inv * sg_ref[...] * (1.0 - lam_init)


def _split_sub_heads(q, group_rows, rows_first_map):
    lane = lax.broadcasted_iota(jnp.int32, q.shape, 1)
    row = lax.broadcasted_iota(jnp.int32, q.shape, 0) % group_rows
    keep = (lane < LANES // 2) == (row < rows_first_map)
    return jnp.where(keep, q, jnp.zeros_like(q))


def _diff_prompt_kernel(slopes_ref, lam_ref, q_ref, k_ref, v_ref, sg_ref, o_ref,
                        kb_ref, vt_ref, rb_ref, rbd_ref, m_ref, l_ref, acc_ref,
                        *, tq, lam_init, scale):
    h = pl.program_id(1)
    qi = pl.program_id(2)
    slope = slopes_ref[h]

    @pl.when(qi == 0)
    def _():
        kb_ref[...] = k_ref[...].astype(_BF16)
        for jb in range(vt_ref.shape[0]):
            vt_ref[jb] = v_ref[jb * tq:(jb + 1) * tq, :].T.astype(_BF16)
        rel = (lax.broadcasted_iota(jnp.int32, (tq, tq), 1)
               - lax.broadcasted_iota(jnp.int32, (tq, tq), 0))
        rb = -slope * rel.astype(_F32)
        rb_ref[...] = rb
        rbd_ref[...] = jnp.where(rel >= 0, rb, NEG_INF)

    qs = (q_ref[...].astype(_F32) * scale).astype(_BF16)
    lane = lax.broadcasted_iota(jnp.int32, qs.shape, 1)
    zero = jnp.zeros_like(qs)
    q_maps = (jnp.where(lane < LANES // 2, qs, zero), jnp.where(lane < LANES // 2, zero, qs))
    q0 = qi * tq

    m_ref[...] = jnp.full_like(m_ref, NEG_INF)
    l_ref[...] = jnp.zeros_like(l_ref)
    acc_ref[...] = jnp.zeros_like(acc_ref)

    def step(kb, vt, bias, shift):
        for mp, qm in enumerate(q_maps):
            s = lax.dot_general(kb, qm, _NT, preferred_element_type=_F32) + bias
            m = m_ref[mp]
            m_new = jnp.maximum(m, jnp.max(s, axis=0, keepdims=True) - shift)
            alpha = jnp.exp(m - m_new)
            p = jnp.exp(s - (m_new + shift))
            l_ref[mp] = alpha * l_ref[mp] + jnp.sum(p, axis=0, keepdims=True)
            m_ref[mp] = m_new
            acc_ref[mp] = alpha * acc_ref[mp] + jnp.dot(vt, p.astype(_BF16),
                                                        preferred_element_type=_F32)

    def body(j, carry):
        k0 = pl.multiple_of(j * tq, tq)
        shift = slope * (q0 - k0).astype(_F32)
        step(kb_ref[pl.ds(k0, tq), :], vt_ref[j], rb_ref[...], shift)
        return carry

    lax.fori_loop(0, qi, body, 0)
    kd = pl.multiple_of(q0, tq)
    step(kb_ref[pl.ds(kd, tq), :], vt_ref[qi], rbd_ref[...], 0.0)
    lam = _diff_lambda(lam_ref, lam_init)
    od = acc_ref[0] / l_ref[0] - lam * (acc_ref[1] / l_ref[1])
    inv = lax.rsqrt(jnp.mean(od * od, axis=0, keepdims=True) + EPS)
    o_ref[...] = ((od * inv).T * sg_ref[...] * (1.0 - lam_init)).astype(o_ref.dtype)


def diff_attn_prompt(q, kv, li, lam_vec, sub_gain, lam_init, bsz, seq_len, n_heads, tq):
    nq = seq_len // tq
    slopes = _alibi_slopes(n_heads)
    return pl.pallas_call(
        functools.partial(_diff_prompt_kernel, tq=tq, lam_init=lam_init,
                          scale=(LANES // 2) ** -0.5),
        grid=(bsz, n_heads, nq),
        in_specs=[pl.BlockSpec(memory_space=pltpu.SMEM),
                  pl.BlockSpec(lam_vec.shape, lambda b, h, i: (0, 0)),
                  pl.BlockSpec((tq, LANES), lambda b, h, i: (b * nq + i, h)),
                  pl.BlockSpec((None, seq_len, LANES), lambda b, h, i: (li, b, h)),
                  pl.BlockSpec((None, seq_len, LANES), lambda b, h, i: (li, b, n_heads + h)),
                  pl.BlockSpec((1, LANES), lambda b, h, i: (0, 0))],
        out_specs=pl.BlockSpec((tq, LANES), lambda b, h, i: (b * nq + i, h)),
        out_shape=jax.ShapeDtypeStruct((bsz * seq_len, n_heads * LANES), _BF16),
        scratch_shapes=[pltpu.VMEM((seq_len, LANES), _BF16), pltpu.VMEM((nq, LANES, tq), _BF16),
                        pltpu.VMEM((tq, tq), _F32), pltpu.VMEM((tq, tq), _F32),
                        pltpu.VMEM((2, 1, tq), _F32), pltpu.VMEM((2, 1, tq), _F32),
                        pltpu.VMEM((2, LANES, tq), _F32)],
        compiler_params=_params("arbitrary", "arbitrary", "arbitrary"),
        name="diff_attn_prompt",
    )(slopes, lam_vec, q, kv, kv, sub_gain.reshape(1, LANES))


def _heads_per_tile(n_heads):
    return SUBLANES if n_heads % SUBLANES == 0 else n_heads


def _diff_sample_kernel(pt_ref, q_ref, bias_ref, biasn_ref, slope_ref, lam_ref, sg_ref, new_ref,
                        *rest, n_heads, steps, page, n_pages, pages_per_step, lam_init, scale):
    del pt_ref
    page_refs = rest[:pages_per_step]
    o_ref, qs_ref, m_ref, l_ref, acc_ref = rest[pages_per_step:]
    p = pl.program_id(1)
    rows = 2 * steps * n_heads

    @pl.when(p == 0)
    def _():
        q = _split_sub_heads(q_ref[...], 2 * steps, steps)
        qs_ref[...] = (q.astype(_F32) * scale).astype(_BF16)
        m_ref[...] = jnp.full_like(m_ref, NEG_INF)
        l_ref[...] = jnp.zeros_like(l_ref)
        acc_ref[...] = jnp.zeros_like(acc_ref)

    ht = _heads_per_tile(n_heads)
    rows_t = 2 * steps * ht

    def attend(kv_refs, shifts, b_ref):
        nk = kv_refs[0].shape[0] * ht
        for t in range(n_heads // ht):
            rs = slice(t * rows_t, (t + 1) * rows_t)
            hs = slice(t * ht, (t + 1) * ht)
            sh = [x if isinstance(x, float) else x[rs] for x in shifts]
            s = [lax.dot_general(qs_ref[rs, :], r[:, 0, hs, :].reshape(nk, LANES).astype(_BF16), _NT,
                                 preferred_element_type=_F32) + b_ref[rs, :] for r in kv_refs]
            m = m_ref[rs, :]
            m_new = functools.reduce(
                jnp.maximum, [jnp.max(si, axis=-1, keepdims=True) - shi for si, shi in zip(s, sh)], m)
            alpha = jnp.exp(m - m_new)
            e = [jnp.exp(si - (m_new + shi)) for si, shi in zip(s, sh)]
            l_ref[rs, :] = alpha * l_ref[rs, :] + sum(jnp.sum(ei, axis=-1, keepdims=True) for ei in e)
            m_ref[rs, :] = m_new
            pv = sum(jnp.dot(ei.astype(_BF16), r[:, 1, hs, :].reshape(nk, LANES).astype(_BF16),
                             preferred_element_type=_F32) for ei, r in zip(e, kv_refs))
            acc_ref[rs, :] = alpha * acc_ref[rs, :] + pv

    slope = slope_ref[...]
    firsts = [(n_pages - (p * pages_per_step + i)) * page for i in range(pages_per_step)]
    attend(page_refs, [slope * f.astype(_F32) for f in firsts], bias_ref)

    @pl.when(p == n_pages // pages_per_step - 1)
    def _():
        attend([new_ref], [0.0], biasn_ref)
        o = acc_ref[...] / l_ref[...]
        lam = _diff_lambda(lam_ref, lam_init)
        od = o - lam * pltpu.roll(o, rows - steps, axis=0)
        o_ref[...] = _sub_norm(od, sg_ref, lam_init)


def diff_attn_sample(q, kv_new, cache, layer, page_table, lam_vec, sub_gain, lam_init,
                     n_heads, steps, pages_per_step):
    db, n_pages = page_table.shape
    page = cache.shape[2]
    hw = n_heads * LANES
    rows = 2 * steps * n_heads
    new_keys = 16
    assert rows % 8 == 0 and n_pages % pages_per_step == 0 and steps <= new_keys
    qr = q.reshape(db, steps, n_heads, 1, LANES).transpose(0, 2, 3, 1, 4)
    qr = jnp.broadcast_to(qr, (db, n_heads, 2, steps, LANES)).reshape(db, rows, LANES)
    new = jnp.pad(kv_new.reshape(db, steps, 2, n_heads, LANES),
                  ((0, 0), (0, new_keys - steps), (0, 0), (0, 0), (0, 0)))
    slope_rows = jnp.repeat(_alibi_slopes(n_heads), 2 * steps).reshape(rows, 1)

    ht = _heads_per_tile(n_heads)

    def bias_table(n_keys, causal):
        r = jnp.arange(rows)[:, None]
        c = jnp.arange(n_keys * ht)[None, :]
        back = r % steps - c // ht
        ok = (r // (2 * steps)) % ht == c % ht
        if causal:
            ok = ok & (back >= 0)
        return jnp.where(ok, -slope_rows * back.astype(_F32), NEG_INF)

    page_spec = lambda i: pl.BlockSpec(
        (None, None, page, 2, n_heads, LANES),
        lambda b, p, pt: (layer, pt[b * n_pages + p * pages_per_step + i], 0, 0, 0, 0))
    const = lambda shape: pl.BlockSpec(shape, lambda b, p, pt: (0,) * len(shape))
    grid_spec = pltpu.PrefetchScalarGridSpec(
        num_scalar_prefetch=1,
        grid=(db, n_pages // pages_per_step),
        in_specs=[pl.BlockSpec((None, rows, LANES), lambda b, p, pt: (b, 0, 0)),
                  const((rows, page * ht)), const((rows, new_keys * ht)),
                  const((rows, 1)), const(lam_vec.shape), const((1, LANES)),
                  pl.BlockSpec((None, new_keys, 2, n_heads, LANES), lambda b, p, pt: (b, 0, 0, 0, 0))]
                 + [page_spec(i) for i in range(pages_per_step)],
        out_specs=pl.BlockSpec((None, rows, LANES), lambda b, p, pt: (b, 0, 0)),
        scratch_shapes=[pltpu.VMEM((rows, LANES), _BF16), pltpu.VMEM((rows, 1), _F32),
                        pltpu.VMEM((rows, 1), _F32), pltpu.VMEM((rows, LANES), _F32)])
    o = pl.pallas_call(
        functools.partial(_diff_sample_kernel, n_heads=n_heads, steps=steps, page=page,
                          n_pages=n_pages, pages_per_step=pages_per_step, lam_init=lam_init,
                          scale=(LANES // 2) ** -0.5),
        grid_spec=grid_spec,
        out_shape=jax.ShapeDtypeStruct((db, rows, LANES), _F32),
        compiler_params=_params("arbitrary", "arbitrary"),
        name="diff_attn_sample",
    )(page_table.reshape(-1), qr, bias_table(page, False), bias_table(new_keys, True), slope_rows,
      lam_vec, sub_gain.reshape(1, LANES), new, *([cache] * pages_per_step))
    o = o.reshape(db, n_heads, 2 * steps, LANES)[:, :, :steps]
    return o.transpose(0, 2, 1, 3).reshape(db * steps, hw).astype(_BF16)


def _dilated_prompt_kernel(slopes_ref, *refs, groups, seq_len, heads_per_group, scale):
    ng = len(groups)
    q_refs, k_refs, v_refs = refs[:ng], refs[ng:2 * ng], refs[2 * ng:3 * ng]
    o_ref, og_ref, lse_ref = refs[3 * ng:]
    j = pl.program_id(1)

    for g, (window, dil) in enumerate(groups):
        blk = window // dil
        sub_len = seq_len // dil
        assert sub_len % blk == 0
        nb = sub_len // blk
        slope = slopes_ref[g * heads_per_group + j]
        qi = lax.broadcasted_iota(jnp.int32, (blk, 2 * blk), 0)
        ci = lax.broadcasted_iota(jnp.int32, (blk, 2 * blk), 1)
        sub_dist = qi + blk - ci
        in_band = (sub_dist >= 0) & (sub_dist <= blk)
        bias = -slope * (sub_dist * dil).astype(_F32)
        q_ref, k_ref, v_ref = q_refs[g], k_refs[g], v_refs[g]

        def body(it, _):
            r = it // nb
            n = it % nb
            start = n * (blk * dil) + r
            prev = jnp.maximum(start - blk * dil, r)
            rows_c = pl.ds(start, blk, stride=dil) if dil > 1 else pl.ds(start, blk)
            rows_p = pl.ds(prev, blk, stride=dil) if dil > 1 else pl.ds(prev, blk)
            qb = q_ref[rows_c, :].astype(_BF16)
            kk = jnp.concatenate([k_ref[rows_p, :], k_ref[rows_c, :]], axis=0).astype(_BF16)
            vv = jnp.concatenate([v_ref[rows_p, :], v_ref[rows_c, :]], axis=0).astype(_BF16)
            s = lax.dot_general(qb, kk, _NT, preferred_element_type=_F32) * scale
            valid = in_band & (ci + (n - 1) * blk >= 0)
            s = jnp.where(valid, s + bias, NEG_INF)
            m = jnp.max(s, axis=-1, keepdims=True)
            e = jnp.exp(s - m)
            l = jnp.sum(e, axis=-1, keepdims=True)
            o = jnp.dot(e.astype(_BF16), vv, preferred_element_type=_F32) / l
            og_ref[g, rows_c, :] = o
            lse_ref[g, rows_c, :] = jnp.broadcast_to(m + jnp.log(l), (blk, LANES))
            return 0

        lax.fori_loop(0, dil * nb, body, 0, unroll=16)

    chunk = 256
    for c in range(seq_len // chunk):
        sl = pl.ds(c * chunk, chunk)
        lses = [lse_ref[g, sl, :] for g in range(ng)]
        mx = functools.reduce(jnp.maximum, lses)
        ws = [jnp.exp(x - mx) for x in lses]
        num = functools.reduce(lambda a, b: a + b, [w * og_ref[g, sl, :] for g, w in enumerate(ws)])
        den = functools.reduce(lambda a, b: a + b, ws)
        o_ref[sl, :] = (num / den).astype(o_ref.dtype)


def dilated_attn_prompt(q, kv, bsz, seq_len, groups, heads_per_group):
    ng = len(groups)
    nh = ng * heads_per_group
    slopes = _alibi_slopes(nh)
    blk_spec = lambda off: pl.BlockSpec((seq_len, LANES), lambda b, j, off=off: (b, off + j))
    in_specs = [pl.BlockSpec(memory_space=pltpu.SMEM)]
    in_specs += [blk_spec(g * heads_per_group) for g in range(ng)]
    in_specs += [blk_spec(g * heads_per_group) for g in range(ng)]
    in_specs += [blk_spec(nh + g * heads_per_group) for g in range(ng)]
    return pl.pallas_call(
        functools.partial(_dilated_prompt_kernel, groups=groups, seq_len=seq_len,
                          heads_per_group=heads_per_group, scale=LANES ** -0.5),
        grid=(bsz, heads_per_group),
        in_specs=in_specs,
        out_specs=pl.BlockSpec((seq_len, LANES), lambda b, j: (b, j)),
        out_shape=jax.ShapeDtypeStruct((bsz * seq_len, heads_per_group * LANES), _BF16),
        scratch_shapes=[pltpu.VMEM((ng, seq_len, LANES), _F32), pltpu.VMEM((ng, seq_len, LANES), _F32)],
        compiler_params=_params("arbitrary", "arbitrary"),
        name="dilated_attn_prompt",
    )(slopes, *([q] * ng), *([kv] * ng), *([kv] * ng))


def _dilated_sample_kernel(q_ref, new_ref, *refs, groups, steps, scale):
    ng = len(groups)
    st_refs, bp_refs, bn_refs = refs[:ng], refs[ng:2 * ng], refs[2 * ng:3 * ng]
    o_ref = refs[3 * ng]
    for t in range(steps):
        ms, ls, accs = [], [], []
        for g, (_, dil) in enumerate(groups):
            sub = t if dil > 1 else 0
            qb = q_ref[t, g].astype(_BF16)
            flat = lambda x: x.reshape(x.shape[0] * x.shape[1], LANES).astype(_BF16)
            kp, vp = flat(st_refs[g][:, sub, 0]), flat(st_refs[g][:, sub, 1])
            kn, vn = flat(new_ref[:, 0, g]), flat(new_ref[:, 1, g])
            sp = lax.dot_general(qb, kp, _NT, preferred_element_type=_F32) * scale + bp_refs[g][t]
            sn = lax.dot_general(qb, kn, _NT, preferred_element_type=_F32) * scale + bn_refs[g][t]
            m = jnp.maximum(jnp.max(sp, axis=-1, keepdims=True), jnp.max(sn, axis=-1, keepdims=True))
            ep = jnp.exp(sp - m)
            en = jnp.exp(sn - m)
            ls.append(jnp.sum(ep, axis=-1, keepdims=True) + jnp.sum(en, axis=-1, keepdims=True))
            accs.append(jnp.dot(ep.astype(_BF16), vp, preferred_element_type=_F32)
                        + jnp.dot(en.astype(_BF16), vn, preferred_element_type=_F32))
            ms.append(m)
        mx = functools.reduce(jnp.maximum, ms)
        ws = [jnp.exp(m - mx) for m in ms]
        num = functools.reduce(lambda a, b: a + b, [w * o for w, o in zip(ws, accs)])
        den = functools.reduce(lambda a, b: a + b, [w * l for w, l in zip(ws, ls)])
        o_ref[t] = num / den


def dilated_attn_sample(q, kv_new, states, layer, groups, heads_per_group, db, steps):
    ng = len(groups)
    hg = heads_per_group
    new_keys = 16
    assert steps <= new_keys and all(steps <= dil or dil == 1 for _, dil in groups)
    q5 = q.reshape(db, steps, ng, hg, LANES)
    new = jnp.pad(kv_new.reshape(db, steps, 2, ng, hg, LANES),
                  ((0, 0), (0, new_keys - steps)) + ((0, 0),) * 4)
    slopes = _alibi_slopes(ng * hg).reshape(ng, hg, 1)
    head = jnp.arange(hg)[:, None]
    t = jnp.arange(steps)[:, None, None]
    views, st_specs, bias_p, bias_n = [], [], [], []
    for g, (window, dil) in enumerate(groups):
        st = states[g]
        assert st.shape[2] == window
        nb = window // dil
        views.append(st.reshape(st.shape[0], db, nb, dil, 2, hg, LANES))
        st_specs.append(pl.BlockSpec((None, None, nb, min(dil, steps), 2, hg, LANES),
                                     lambda b: (layer, b, 0, 0, 0, 0, 0)))
        col = jnp.arange(nb * hg)[None, :]
        m, hp = col // hg, col % hg
        coln = jnp.arange(new_keys * hg)[None, :]
        tn, hn = coln // hg, coln % hg
        if dil == 1:
            back_p, ok_p = nb + t - m, (hp == head) & (m >= t)
            back_n, ok_n = t - tn, (hn == head) & (tn <= t)
        else:
            back_p, ok_p = nb - m + 0 * t, (hp == head) & (t >= 0)
            back_n, ok_n = 0 * (t - tn), (hn == head) & (tn == t)
        bias_p.append(jnp.where(ok_p, -slopes[g] * (back_p * dil).astype(_F32), NEG_INF))
        bias_n.append(jnp.where(ok_n, -slopes[g] * (back_n * dil).astype(_F32), NEG_INF))
    full = lambda x: pl.BlockSpec(x.shape, lambda b: (0,) * x.ndim)
    o = pl.pallas_call(
        functools.partial(_dilated_sample_kernel, groups=groups, steps=steps, scale=LANES ** -0.5),
        grid=(db,),
        in_specs=[pl.BlockSpec((None, steps, ng, hg, LANES), lambda b: (b, 0, 0, 0, 0)),
                  pl.BlockSpec((None, new_keys, 2, ng, hg, LANES), lambda b: (b, 0, 0, 0, 0, 0))]
                 + st_specs + [full(x) for x in bias_p] + [full(x) for x in bias_n],
        out_specs=pl.BlockSpec((None, steps, hg, LANES), lambda b: (b, 0, 0, 0)),
        out_shape=jax.ShapeDtypeStruct((db, steps, hg, LANES), _F32),
        compiler_params=_params("arbitrary"),
        name="dilated_attn_sample",
    )(q5, new, *views, *bias_p, *bias_n)
    return o.reshape(db * steps, hg * LANES).astype(_BF16)


def _shift_state_kernel(cur_ref, nxt_ref, new_ref, out_ref, *, steps):
    rows = cur_ref.shape[0]
    out_ref[0:rows - steps] = cur_ref[steps:rows]
    last = pl.program_id(2) == pl.num_programs(2) - 1

    @pl.when(last)
    def _():
        out_ref[rows - steps:rows] = new_ref[...]

    @pl.when(jnp.logical_not(last))
    def _():
        out_ref[rows - steps:rows] = nxt_ref[...]


def shift_window_state(state, new):
    layers, db, window = state.shape[:3]
    steps = new.shape[2]
    rows = min(window, 512)
    assert window % rows == 0 and rows % steps == 0 and rows > steps
    chunks = window // rows
    per = rows // steps
    tile = state.shape[3:]
    nil = (0,) * len(tile)
    return pl.pallas_call(
        functools.partial(_shift_state_kernel, steps=steps),
        grid=(layers, db, chunks),
        in_specs=[pl.BlockSpec((None, None, rows) + tile, lambda l, b, c: (l, b, c) + nil),
                  pl.BlockSpec((None, None, steps) + tile,
                               lambda l, b, c: (l, b, jnp.minimum(c + 1, chunks - 1) * per) + nil),
                  pl.BlockSpec((None, None, steps) + tile, lambda l, b, c: (l, b, 0) + nil)],
        out_specs=pl.BlockSpec((None, None, rows) + tile, lambda l, b, c: (l, b, c) + nil),
        out_shape=jax.ShapeDtypeStruct(state.shape, state.dtype),
        compiler_params=_params("arbitrary", "arbitrary", "arbitrary"),
        name="shift_window_state",
    )(state, state, new)


def kernel(x_prompt, x_sample, cache_kv_diff, state_kv_w128, state_kv_w512, state_kv_w2048,
           state_conv, page_table, attn_norm, ffn_norm, a_w_qkv, a_w_o, a_q_gain, a_k_gain,
           a_lambda, a_sub_gain, b_w_qkv, b_w_o, b_q_gain, b_k_gain, ffn_w_gate, ffn_w_up,
           ffn_w_down, ffn_conv_w, ffn_conv_b):
    bsz, seq_len, d_model = x_prompt.shape
    db, steps, _ = x_sample.shape
    depth = attn_norm.shape[0]
    d_ff = ffn_w_gate.shape[2]
    a_heads = a_w_qkv.shape[2] // (3 * LANES)
    ng = len(B_GROUPS)
    b_heads = b_w_qkv.shape[2] // (3 * LANES)
    hg = b_heads // ng
    aw = a_heads * LANES
    bw = b_heads * LANES
    mp, ms = bsz * seq_len, db * steps
    tm, tn = min(1024, seq_len), 512
    tm_down = min(512, seq_len)
    tq = min(512, seq_len)
    pages_per_step = 4
    win_states = (state_kv_w128, state_kv_w512, state_kv_w2048)
    cache = cache_kv_diff

    xp = x_prompt.reshape(mp, d_model)
    xs = x_sample.reshape(ms, d_model)
    n_a_layers = (depth + 1) // 2
    kv_diff_all, kv_diff_s = jnp.zeros((n_a_layers, mp, 2 * aw), _F32), []
    win_p = [[] for _ in B_GROUPS]
    win_s = [[] for _ in B_GROUPS]
    conv_p, conv_s = [], []

    for i in range(depth):
        hp = rmsnorm_bf16(xp, attn_norm[i], tm)
        hs = rmsnorm_bf16(xs, attn_norm[i], ms)
        if i % 2 == 0:
            a = i // 2
            lam_init = 0.8 - 0.6 * math.exp(-0.3 * i)
            w = a_w_qkv
            qp = matmul_headnorm(hp, w, a, a_q_gain[a], 0, aw, aw, LANES // 2, _BF16, tm, tn)
            kv_diff_all = matmul_headnorm(hp, w, a, a_k_gain[a], aw, 2 * aw, aw, LANES // 2, _F32,
                                          tm, tn, stack=(kv_diff_all, a))
            qs = matmul_headnorm(hs, w, a, a_q_gain[a], 0, aw, aw, LANES // 2, _BF16, ms, tn)
            kvs = matmul_headnorm(hs, w, a, a_k_gain[a], aw, 2 * aw, aw, LANES // 2, _F32, ms, tn)
            op = diff_attn_prompt(qp, kv_diff_all, a, a_lambda[a], a_sub_gain[a], lam_init, bsz,
                                  seq_len, a_heads, tq)
            os_ = diff_attn_sample(qs, kvs, cache, a, page_table, a_lambda[a], a_sub_gain[a],
                                   lam_init, a_heads, steps, pages_per_step)
            xp = matmul_residual(op, a_w_o, a, xp, tm, tn, after=os_)
            xs = matmul_residual(os_, a_w_o, a, xs, ms, tn)
            kv_diff_s.append(kvs.reshape(db, steps, 2, a_heads, LANES))
        else:
            b = i // 2
            w = b_w_qkv
            qp = matmul_headnorm(hp, w, b, b_q_gain[b], 0, bw, bw, LANES, _F32, tm, tn)
            kvp = matmul_headnorm(hp, w, b, b_k_gain[b], bw, 2 * bw, bw, LANES, _F32, tm, tn)
            qs = matmul_headnorm(hs, w, b, b_q_gain[b], 0, bw, bw, LANES, _F32, ms, tn)
            kvs = matmul_headnorm(hs, w, b, b_k_gain[b], bw, 2 * bw, bw, LANES, _F32, ms, tn)
            op = dilated_attn_prompt(qp, kvp, bsz, seq_len, B_GROUPS, hg)
            os_ = dilated_attn_sample(qs, kvs, win_states, b, B_GROUPS, hg, db, steps)
            xp = matmul_residual(op, b_w_o, b, xp, tm, tn)
            xs = matmul_residual(os_, b_w_o, b, xs, ms, tn)
            kp3 = kvp.reshape(bsz, seq_len, 2 * bw)
            ks4 = kvs.reshape(db, steps, 2, ng, hg, LANES)
            gw = hg * LANES
            for g, (window, _) in enumerate(B_GROUPS):
                keep = min(window, seq_len)
                kept = [kp3[:, seq_len - keep:, c * bw + g * gw:c * bw + (g + 1) * gw] for c in (0, 1)]
                win_p[g].append(jnp.stack(kept, axis=2).reshape(bsz, keep, 2, hg, LANES))
                win_s[g].append(ks4[:, :, :, g])
        hp = rmsnorm_bf16(xp, ffn_norm[i], tm)
        hs = rmsnorm_bf16(xs, ffn_norm[i], ms)
        actp, tail = ffn_gate_up_prompt(hp, ffn_w_gate, ffn_w_up, i, ffn_conv_w[i], ffn_conv_b[i],
                                        seq_len, tm, tn)
        acts, gs = ffn_gate_up_sample(hs, ffn_w_gate, ffn_w_up, i, ffn_conv_w[i], ffn_conv_b[i],
                                      state_conv[i], tn)
        xp = matmul_residual(actp, ffn_w_down, i, xp, tm_down, tn)
        xs = matmul_residual(acts, ffn_w_down, i, xs, ms, tn)
        conv_p.append(tail[:, 6:])
        conv_s.append(gs.reshape(db, steps, d_ff)[:, steps - 2:])

    new_state = [shift_window_state(win_states[g], jnp.stack(win_s[g])) for g in range(ng)]
    return (xp.reshape(bsz, seq_len, d_model), xs.reshape(db, steps, d_model),
            kv_diff_all.reshape(n_a_layers, bsz, seq_len, 2, a_heads, LANES), jnp.stack(kv_diff_s),
            jnp.stack(win_p[0]), new_state[0], jnp.stack(win_p[1]), new_state[1],
            jnp.stack(win_p[2]), new_state[2], jnp.stack(conv_p), jnp.stack(conv_s))
```

```python
import functools
import math

import jax
import jax.numpy as jnp
from jax import lax
from jax.experimental import pallas as pl
from jax.experimental.pallas import tpu as pltpu

EPS = 1e-6
NEG_INF = -1e30
LANES = 128
SUBLANES = 8
MXU_COLS = 256
B_GROUPS = ((128, 1), (512, 4), (2048, 16))
VMEM_LIMIT = 56 * 1024 * 1024

_BF16 = jnp.bfloat16
_F32 = jnp.float32
_NT = (((1,), (1,)), ((), ()))


def _params(*sem):
    return pltpu.CompilerParams(dimension_semantics=sem, vmem_limit_bytes=VMEM_LIMIT)


def _alibi_slopes(n):
    return 2.0 ** (-8.0 * jnp.arange(1, n + 1, dtype=_F32) / n)


def _rmsnorm_kernel(x_ref, g_ref, o_ref):
    x = x_ref[...]
    inv = lax.rsqrt(jnp.mean(x * x, axis=-1, keepdims=True) + EPS)
    o_ref[...] = (x * inv * g_ref[...]).astype(o_ref.dtype)


def rmsnorm_bf16(x, gain, tm):
    m, d = x.shape
    return pl.pallas_call(
        _rmsnorm_kernel,
        grid=(pl.cdiv(m, tm),),
        in_specs=[pl.BlockSpec((tm, d), lambda i: (i, 0)),
                  pl.BlockSpec((1, d), lambda i: (0, 0))],
        out_specs=pl.BlockSpec((tm, d), lambda i: (i, 0)),
        out_shape=jax.ShapeDtypeStruct((m, d), _BF16),
        compiler_params=_params("parallel"),
        name="rmsnorm",
    )(x, gain.reshape(1, d))


def _cast_weight(w_ref, wb_ref):
    @pl.when(pl.program_id(1) == 0)
    def _():
        wb_ref[...] = w_ref[...].astype(_BF16)


def _col_chunks(tn):
    step = min(tn, MXU_COLS)
    return [slice(c, c + step) for c in range(0, tn, step)]


def _mm_res_kernel(a_ref, w_ref, r_ref, *rest):
    o_ref, wb_ref = rest[-2:]
    _cast_weight(w_ref, wb_ref)
    for sl in _col_chunks(o_ref.shape[1]):
        acc = jnp.dot(a_ref[...], wb_ref[:, sl], preferred_element_type=_F32)
        o_ref[:, sl] = r_ref[:, sl] + acc


def matmul_residual(a, w, li, res, tm, tn, after=None):
    m, k = a.shape
    n = w.shape[2]
    in_specs = [pl.BlockSpec((tm, k), lambda j, i: (i, 0)),
                pl.BlockSpec((None, k, tn), lambda j, i: (li, 0, j)),
                pl.BlockSpec((tm, tn), lambda j, i: (i, j))]
    args = [a, w, res]
    if after is not None:
        in_specs.append(pl.BlockSpec(memory_space=pl.ANY))
        args.append(after)
    return pl.pallas_call(
        _mm_res_kernel,
        grid=(n // tn, pl.cdiv(m, tm)),
        in_specs=in_specs,
        out_specs=pl.BlockSpec((tm, tn), lambda j, i: (i, j)),
        out_shape=jax.ShapeDtypeStruct((m, n), _F32),
        scratch_shapes=[pltpu.VMEM((k, tn), _BF16)],
        compiler_params=_params("arbitrary", "arbitrary"),
        name="matmul_residual",
    )(*args)


def _head_rms(c, gain, group):
    sq = c * c
    if group == LANES:
        ms = jnp.sum(sq, axis=-1, keepdims=True) * (1.0 / group)
    else:
        lane = lax.broadcasted_iota(jnp.int32, c.shape, 1)
        low = lane < group
        s_lo = jnp.sum(jnp.where(low, sq, 0.0), axis=-1, keepdims=True)
        s_hi = jnp.sum(jnp.where(low, 0.0, sq), axis=-1, keepdims=True)
        ms = jnp.where(low, s_lo, s_hi) * (1.0 / group)
    return c * lax.rsqrt(ms + EPS) * gain


def _mm_norm_kernel(a_ref, w_ref, g_ref, *rest, group, norm_blocks, n_blocks):
    o_ref, wb_ref = rest[-2:]
    _cast_weight(w_ref, wb_ref)
    chunks = _col_chunks(o_ref.shape[1])

    def normed():
        g = g_ref[...]
        for sl in chunks:
            acc = jnp.dot(a_ref[...], wb_ref[:, sl], preferred_element_type=_F32)
            for c in range(sl.start, sl.stop, LANES):
                o_ref[:, c:c + LANES] = _head_rms(acc[:, c - sl.start:c - sl.start + LANES],
                                                  g, group).astype(o_ref.dtype)

    def raw():
        for sl in chunks:
            acc = jnp.dot(a_ref[...], wb_ref[:, sl], preferred_element_type=_F32)
            o_ref[:, sl] = acc.astype(o_ref.dtype)

    if norm_blocks >= n_blocks:
        normed()
    elif norm_blocks == 0:
        raw()
    else:
        pl.when(pl.program_id(0) < norm_blocks)(normed)
        pl.when(pl.program_id(0) >= norm_blocks)(raw)


def matmul_headnorm(a, w, li, gain, col0, ncols, norm_cols, group, out_dtype, tm, tn, stack=None):
    m, k = a.shape
    assert col0 % tn == 0 and ncols % tn == 0 and norm_cols % tn == 0
    assert LANES % group == 0 and tn % LANES == 0
    off = col0 // tn
    n_blocks = ncols // tn
    g = jnp.tile(gain.astype(_F32), LANES // group).reshape(1, LANES)
    in_specs = [pl.BlockSpec((tm, k), lambda j, i: (i, 0)),
                pl.BlockSpec((None, k, tn), lambda j, i: (li, 0, j + off)),
                pl.BlockSpec((1, LANES), lambda j, i: (0, 0))]
    args = [a, w, g]
    aliases = {}
    if stack is None:
        out_spec = pl.BlockSpec((tm, tn), lambda j, i: (i, j))
        out_shape = jax.ShapeDtypeStruct((m, ncols), out_dtype)
    else:
        buf, slot = stack
        assert buf.shape[1:] == (m, ncols) and buf.dtype == out_dtype
        out_spec = pl.BlockSpec((None, tm, tn), lambda j, i: (slot, i, j))
        out_shape = jax.ShapeDtypeStruct(buf.shape, out_dtype)
        in_specs.append(pl.BlockSpec(memory_space=pl.ANY))
        args.append(buf)
        aliases = {3: 0}
    return pl.pallas_call(
        functools.partial(_mm_norm_kernel, group=group, norm_blocks=norm_cols // tn,
                          n_blocks=n_blocks),
        grid=(n_blocks, pl.cdiv(m, tm)),
        in_specs=in_specs,
        out_specs=out_spec,
        out_shape=out_shape,
        input_output_aliases=aliases,
        scratch_shapes=[pltpu.VMEM((k, tn), _BF16)],
        compiler_params=_params("arbitrary", "arbitrary"),
        name="matmul_headnorm",
    )(*args)


def _gated(g, g1, g2, u, cw, cb):
    c = cb + g2 * cw[0:1] + g1 * cw[1:2] + g * cw[2:3]
    return (c * jax.nn.sigmoid(c) * u).astype(_BF16)


def _ffn_prompt_kernel(a_ref, wg_ref, wu_ref, cw_ref, cb_ref, act_ref, tail_ref,
                       wgb_ref, wub_ref, carry_ref, *, tiles_per_seq):
    _cast_weight(wg_ref, wgb_ref)
    _cast_weight(wu_ref, wub_ref)
    tm = a_ref.shape[0]

    @pl.when(pl.program_id(1) % tiles_per_seq == 0)
    def _():
        carry_ref[...] = jnp.zeros_like(carry_ref)

    for sl in _col_chunks(act_ref.shape[1]):
        g = jnp.dot(a_ref[...], wgb_ref[:, sl], preferred_element_type=_F32)
        u = jnp.dot(a_ref[...], wub_ref[:, sl], preferred_element_type=_F32)
        prev = carry_ref[:, sl]
        row = lax.broadcasted_iota(jnp.int32, g.shape, 0)
        g1 = jnp.where(row == 0, prev[7:8], pltpu.roll(g, 1, axis=0))
        g2 = jnp.where(row == 0, prev[6:7], jnp.where(row == 1, prev[7:8], pltpu.roll(g, 2, axis=0)))
        act_ref[:, sl] = _gated(g, g1, g2, u, cw_ref[:, sl], cb_ref[:, sl])
        tail = g[tm - 8:tm]
        carry_ref[:, sl] = tail
        tail_ref[:, sl] = tail


def ffn_gate_up_prompt(a, w_gate, w_up, li, conv_w, conv_b, seq_len, tm, tn):
    m, k = a.shape
    f = w_gate.shape[2]
    assert seq_len % tm == 0 and m % seq_len == 0 and f % tn == 0
    tps = seq_len // tm
    return pl.pallas_call(
        functools.partial(_ffn_prompt_kernel, tiles_per_seq=tps),
        grid=(f // tn, m // tm),
        in_specs=[pl.BlockSpec((tm, k), lambda j, i: (i, 0)),
                  pl.BlockSpec((None, k, tn), lambda j, i: (li, 0, j)),
                  pl.BlockSpec((None, k, tn), lambda j, i: (li, 0, j)),
                  pl.BlockSpec((3, tn), lambda j, i: (0, j)),
                  pl.BlockSpec((1, tn), lambda j, i: (0, j))],
        out_specs=[pl.BlockSpec((tm, tn), lambda j, i: (i, j)),
                   pl.BlockSpec((None, 8, tn), lambda j, i: (i // tps, 0, j))],
        out_shape=[jax.ShapeDtypeStruct((m, f), _BF16),
                   jax.ShapeDtypeStruct((m // seq_len, 8, f), _F32)],
        scratch_shapes=[pltpu.VMEM((k, tn), _BF16), pltpu.VMEM((k, tn), _BF16),
                        pltpu.VMEM((8, tn), _F32)],
        compiler_params=_params("arbitrary", "arbitrary"),
        name="ffn_gate_up_prompt",
    )(a, w_gate, w_up, conv_w, conv_b.reshape(1, f))


def _ffn_sample_kernel(a_ref, wg_ref, wu_ref, cw_ref, cb_ref, s0_ref, s1_ref, act_ref, g_ref,
                       *, steps):
    a = a_ref[...]
    g = jnp.dot(a, wg_ref[...].astype(_BF16), preferred_element_type=_F32)
    u = jnp.dot(a, wu_ref[...].astype(_BF16), preferred_element_type=_F32)
    pos = lax.broadcasted_iota(jnp.int32, g.shape, 0) % steps
    s0 = s0_ref[...]
    s1 = s1_ref[...]
    g1 = jnp.where(pos == 0, s1, pltpu.roll(g, 1, axis=0))
    g2 = jnp.where(pos == 0, s0, jnp.where(pos == 1, s1, pltpu.roll(g, 2, axis=0)))
    act_ref[...] = _gated(g, g1, g2, u, cw_ref[...], cb_ref[...])
    g_ref[...] = g


def ffn_gate_up_sample(a, w_gate, w_up, li, conv_w, conv_b, state, tn):
    m, k = a.shape
    f = w_gate.shape[2]
    db = state.shape[0]
    steps = m // db
    assert steps >= 2 and f % tn == 0
    s0 = jnp.repeat(state[:, 0], steps, axis=0)
    s1 = jnp.repeat(state[:, 1], steps, axis=0)
    col = lambda j: (0, j)
    return pl.pallas_call(
        functools.partial(_ffn_sample_kernel, steps=steps),
        grid=(f // tn,),
        in_specs=[pl.BlockSpec((m, k), lambda j: (0, 0)),
                  pl.BlockSpec((None, k, tn), lambda j: (li, 0, j)),
                  pl.BlockSpec((None, k, tn), lambda j: (li, 0, j)),
                  pl.BlockSpec((3, tn), col), pl.BlockSpec((1, tn), col),
                  pl.BlockSpec((m, tn), col), pl.BlockSpec((m, tn), col)],
        out_specs=[pl.BlockSpec((m, tn), col), pl.BlockSpec((m, tn), col)],
        out_shape=[jax.ShapeDtypeStruct((m, f), _BF16), jax.ShapeDtypeStruct((m, f), _F32)],
        compiler_params=_params("arbitrary"),
        name="ffn_gate_up_sample",
    )(a, w_gate, w_up, conv_w, conv_b.reshape(1, f), s0, s1)


def _diff_lambda(lam_ref, lam_init):
    lv = lam_ref[...]
    a = jnp.sum(lv[0:1] * lv[1:2], axis=-1, keepdims=True)
    b = jnp.sum(lv[2:3] * lv[3:4], axis=-1, keepdims=True)
    return jnp.exp(a) - jnp.exp(b) + lam_init


def _sub_norm(o, sg_ref, lam_init):
    inv = lax.rsqrt(jnp.mean(o * o, axis=-1, keepdims=True) + EPS)
    return o * inv * sg_ref[...] * (1.0 - lam_init)


def _split_sub_heads(q, group_rows, rows_first_map):
    lane = lax.broadcasted_iota(jnp.int32, q.shape, 1)
    row = lax.broadcasted_iota(jnp.int32, q.shape, 0) % group_rows
    keep = (lane < LANES // 2) == (row < rows_first_map)
    return jnp.where(keep, q, jnp.zeros_like(q))


def _diff_prompt_kernel(slopes_ref, lam_ref, q_ref, k_ref, v_ref, sg_ref, o_ref,
                        kb_ref, vt_ref, rb_ref, rbd_ref, m_ref, l_ref, acc_ref,
                        *, tq, lam_init, scale):
    h = pl.program_id(1)
    qi = pl.program_id(2)
    slope = slopes_ref[h]

    @pl.when(qi == 0)
    def _():
        kb_ref[...] = k_ref[...].astype(_BF16)
        for jb in range(vt_ref.shape[0]):
            vt_ref[jb] = v_ref[jb * tq:(jb + 1) * tq, :].T.astype(_BF16)
        rel = (lax.broadcasted_iota(jnp.int32, (tq, tq), 1)
               - lax.broadcasted_iota(jnp.int32, (tq, tq), 0))
        rb = -slope * rel.astype(_F32)
        rb_ref[...] = rb
        rbd_ref[...] = jnp.where(rel >= 0, rb, NEG_INF)

    qs = (q_ref[...].astype(_F32) * scale).astype(_BF16)
    lane = lax.broadcasted_iota(jnp.int32, qs.shape, 1)
    zero = jnp.zeros_like(qs)
    q_maps = (jnp.where(lane < LANES // 2, qs, zero), jnp.where(lane < LANES // 2, zero, qs))
    q0 = qi * tq

    m_ref[...] = jnp.full_like(m_ref, NEG_INF)
    l_ref[...] = jnp.zeros_like(l_ref)
    acc_ref[...] = jnp.zeros_like(acc_ref)

    def step(kb, vt, bias, shift):
        for mp, qm in enumerate(q_maps):
            s = lax.dot_general(kb, qm, _NT, preferred_element_type=_F32) + bias
            m = m_ref[mp]
            m_new = jnp.maximum(m, jnp.max(s, axis=0, keepdims=True) - shift)
            alpha = jnp.exp(m - m_new)
            p = jnp.exp(s - (m_new + shift))
            l_ref[mp] = alpha * l_ref[mp] + jnp.sum(p, axis=0, keepdims=True)
            m_ref[mp] = m_new
            acc_ref[mp] = alpha * acc_ref[mp] + jnp.dot(vt, p.astype(_BF16),
                                                        preferred_element_type=_F32)

    def body(j, carry):
        k0 = pl.multiple_of(j * tq, tq)
        shift = slope * (q0 - k0).astype(_F32)
        step(kb_ref[pl.ds(k0, tq), :], vt_ref[j], rb_ref[...], shift)
        return carry

    lax.fori_loop(0, qi, body, 0)
    kd = pl.multiple_of(q0, tq)
    step(kb_ref[pl.ds(kd, tq), :], vt_ref[qi], rbd_ref[...], 0.0)
    lam = _diff_lambda(lam_ref, lam_init)
    od = acc_ref[0] / l_ref[0] - lam * (acc_ref[1] / l_ref[1])
    inv = lax.rsqrt(jnp.mean(od * od, axis=0, keepdims=True) + EPS)
    o_ref[...] = ((od * inv).T * sg_ref[...] * (1.0 - lam_init)).astype(o_ref.dtype)


def diff_attn_prompt(q, kv, li, lam_vec, sub_gain, lam_init, bsz, seq_len, n_heads, tq):
    nq = seq_len // tq
    slopes = _alibi_slopes(n_heads)
    return pl.pallas_call(
        functools.partial(_diff_prompt_kernel, tq=tq, lam_init=lam_init,
                          scale=(LANES // 2) ** -0.5),
        grid=(bsz, n_heads, nq),
        in_specs=[pl.BlockSpec(memory_space=pltpu.SMEM),
                  pl.BlockSpec(lam_vec.shape, lambda b, h, i: (0, 0)),
                  pl.BlockSpec((tq, LANES), lambda b, h, i: (b * nq + i, h)),
                  pl.BlockSpec((None, seq_len, LANES), lambda b, h, i: (li, b, h)),
                  pl.BlockSpec((None, seq_len, LANES), lambda b, h, i: (li, b, n_heads + h)),
                  pl.BlockSpec((1, LANES), lambda b, h, i: (0, 0))],
        out_specs=pl.BlockSpec((tq, LANES), lambda b, h, i: (b * nq + i, h)),
        out_shape=jax.ShapeDtypeStruct((bsz * seq_len, n_heads * LANES), _BF16),
        scratch_shapes=[pltpu.VMEM((seq_len, LANES), _BF16), pltpu.VMEM((nq, LANES, tq), _BF16),
                        pltpu.VMEM((tq, tq), _F32), pltpu.VMEM((tq, tq), _F32),
                        pltpu.VMEM((2, 1, tq), _F32), pltpu.VMEM((2, 1, tq), _F32),
                        pltpu.VMEM((2, LANES, tq), _F32)],
        compiler_params=_params("arbitrary", "arbitrary", "arbitrary"),
        name="diff_attn_prompt",
    )(slopes, lam_vec, q, kv, kv, sub_gain.reshape(1, LANES))


def _heads_per_tile(n_heads):
    return SUBLANES if n_heads % SUBLANES == 0 else n_heads


def _diff_sample_kernel(pt_ref, q_ref, bias_ref, biasn_ref, slope_ref, lam_ref, sg_ref, new_ref,
                        *rest, n_heads, steps, page, n_pages, pages_per_step, lam_init, scale):
    del pt_ref
    page_refs = rest[:pages_per_step]
    o_ref, qs_ref, m_ref, l_ref, acc_ref = rest[pages_per_step:]
    p = pl.program_id(1)
    rows = 2 * steps * n_heads

    @pl.when(p == 0)
    def _():
        q = _split_sub_heads(q_ref[...], 2 * steps, steps)
        qs_ref[...] = (q.astype(_F32) * scale).astype(_BF16)
        m_ref[...] = jnp.full_like(m_ref, NEG_INF)
        l_ref[...] = jnp.zeros_like(l_ref)
        acc_ref[...] = jnp.zeros_like(acc_ref)

    ht = _heads_per_tile(n_heads)
    rows_t = 2 * steps * ht

    def attend(kv_refs, shifts, b_ref):
        nk = kv_refs[0].shape[0] * ht
        for t in range(n_heads // ht):
            rs = slice(t * rows_t, (t + 1) * rows_t)
            hs = slice(t * ht, (t + 1) * ht)
            sh = [x if isinstance(x, float) else x[rs] for x in shifts]
            s = [lax.dot_general(qs_ref[rs, :], r[:, 0, hs, :].reshape(nk, LANES).astype(_BF16), _NT,
                                 preferred_element_type=_F32) + b_ref[rs, :] for r in kv_refs]
            m = m_ref[rs, :]
            m_new = functools.reduce(
                jnp.maximum, [jnp.max(si, axis=-1, keepdims=True) - shi for si, shi in zip(s, sh)], m)
            alpha = jnp.exp(m - m_new)
            e = [jnp.exp(si - (m_new + shi)) for si, shi in zip(s, sh)]
            l_ref[rs, :] = alpha * l_ref[rs, :] + sum(jnp.sum(ei, axis=-1, keepdims=True) for ei in e)
            m_ref[rs, :] = m_new
            pv = sum(jnp.dot(ei.astype(_BF16), r[:, 1, hs, :].reshape(nk, LANES).astype(_BF16),
                             preferred_element_type=_F32) for ei, r in zip(e, kv_refs))
            acc_ref[rs, :] = alpha * acc_ref[rs, :] + pv

    slope = slope_ref[...]
    firsts = [(n_pages - (p * pages_per_step + i)) * page for i in range(pages_per_step)]
    attend(page_refs, [slope * f.astype(_F32) for f in firsts], bias_ref)

    @pl.when(p == n_pages // pages_per_step - 1)
    def _():
        attend([new_ref], [0.0], biasn_ref)
        o = acc_ref[...] / l_ref[...]
        lam = _diff_lambda(lam_ref, lam_init)
        od = o - lam * pltpu.roll(o, rows - steps, axis=0)
        o_ref[...] = _sub_norm(od, sg_ref, lam_init)


def diff_attn_sample(q, kv_new, cache, layer, page_table, lam_vec, sub_gain, lam_init,
                     n_heads, steps, pages_per_step):
    db, n_pages = page_table.shape
    page = cache.shape[2]
    hw = n_heads * LANES
    rows = 2 * steps * n_heads
    new_keys = 16
    assert rows % 8 == 0 and n_pages % pages_per_step == 0 and steps <= new_keys
    qr = q.reshape(db, steps, n_heads, 1, LANES).transpose(0, 2, 3, 1, 4)
    qr = jnp.broadcast_to(qr, (db, n_heads, 2, steps, LANES)).reshape(db, rows, LANES)
    new = jnp.pad(kv_new.reshape(db, steps, 2, n_heads, LANES),
                  ((0, 0), (0, new_keys - steps), (0, 0), (0, 0), (0, 0)))
    slope_rows = jnp.repeat(_alibi_slopes(n_heads), 2 * steps).reshape(rows, 1)

    ht = _heads_per_tile(n_heads)

    def bias_table(n_keys, causal):
        r = jnp.arange(rows)[:, None]
        c = jnp.arange(n_keys * ht)[None, :]
        back = r % steps - c // ht
        ok = (r // (2 * steps)) % ht == c % ht
        if causal:
            ok = ok & (back >= 0)
        return jnp.where(ok, -slope_rows * back.astype(_F32), NEG_INF)

    page_spec = lambda i: pl.BlockSpec(
        (None, None, page, 2, n_heads, LANES),
        lambda b, p, pt: (layer, pt[b * n_pages + p * pages_per_step + i], 0, 0, 0, 0))
    const = lambda shape: pl.BlockSpec(shape, lambda b, p, pt: (0,) * len(shape))
    grid_spec = pltpu.PrefetchScalarGridSpec(
        num_scalar_prefetch=1,
        grid=(db, n_pages // pages_per_step),
        in_specs=[pl.BlockSpec((None, rows, LANES), lambda b, p, pt: (b, 0, 0)),
                  const((rows, page * ht)), const((rows, new_keys * ht)),
                  const((rows, 1)), const(lam_vec.shape), const((1, LANES)),
                  pl.BlockSpec((None, new_keys, 2, n_heads, LANES), lambda b, p, pt: (b, 0, 0, 0, 0))]
                 + [page_spec(i) for i in range(pages_per_step)],
        out_specs=pl.BlockSpec((None, rows, LANES), lambda b, p, pt: (b, 0, 0)),
        scratch_shapes=[pltpu.VMEM((rows, LANES), _BF16), pltpu.VMEM((rows, 1), _F32),
                        pltpu.VMEM((rows, 1), _F32), pltpu.VMEM((rows, LANES), _F32)])
    o = pl.pallas_call(
        functools.partial(_diff_sample_kernel, n_heads=n_heads, steps=steps, page=page,
                          n_pages=n_pages, pages_per_step=pages_per_step, lam_init=lam_init,
                          scale=(LANES // 2) ** -0.5),
        grid_spec=grid_spec,
        out_shape=jax.ShapeDtypeStruct((db, rows, LANES), _F32),
        compiler_params=_params("arbitrary", "arbitrary"),
        name="diff_attn_sample",
    )(page_table.reshape(-1), qr, bias_table(page, False), bias_table(new_keys, True), slope_rows,
      lam_vec, sub_gain.reshape(1, LANES), new, *([cache] * pages_per_step))
    o = o.reshape(db, n_heads, 2 * steps, LANES)[:, :, :steps]
    return o.transpose(0, 2, 1, 3).reshape(db * steps, hw).astype(_BF16)


def _dilated_prompt_kernel(slopes_ref, *refs, groups, seq_len, heads_per_group, scale):
    ng = len(groups)
    q_refs, k_refs, v_refs = refs[:ng], refs[ng:2 * ng], refs[2 * ng:3 * ng]
    o_ref, og_ref, lse_ref = refs[3 * ng:]
    j = pl.program_id(1)

    for g, (window, dil) in enumerate(groups):
        blk = window // dil
        sub_len = seq_len // dil
        assert sub_len % blk == 0
        nb = sub_len // blk
        slope = slopes_ref[g * heads_per_group + j]
        qi = lax.broadcasted_iota(jnp.int32, (blk, 2 * blk), 0)
        ci = lax.broadcasted_iota(jnp.int32, (blk, 2 * blk), 1)
        sub_dist = qi + blk - ci
        in_band = (sub_dist >= 0) & (sub_dist <= blk)
        bias = -slope * (sub_dist * dil).astype(_F32)
        q_ref, k_ref, v_ref = q_refs[g], k_refs[g], v_refs[g]

        def body(it, _):
            r = it // nb
            n = it % nb
            start = n * (blk * dil) + r
            prev = jnp.maximum(start - blk * dil, r)
            rows_c = pl.ds(start, blk, stride=dil) if dil > 1 else pl.ds(start, blk)
            rows_p = pl.ds(prev, blk, stride=dil) if dil > 1 else pl.ds(prev, blk)
            qb = q_ref[rows_c, :].astype(_BF16)
            kk = jnp.concatenate([k_ref[rows_p, :], k_ref[rows_c, :]], axis=0).astype(_BF16)
            vv = jnp.concatenate([v_ref[rows_p, :], v_ref[rows_c, :]], axis=0).astype(_BF16)
            s = lax.dot_general(qb, kk, _NT, preferred_element_type=_F32) * scale
            valid = in_band & (ci + (n - 1) * blk >= 0)
            s = jnp.where(valid, s + bias, NEG_INF)
            m = jnp.max(s, axis=-1, keepdims=True)
            e = jnp.exp(s - m)
            l = jnp.sum(e, axis=-1, keepdims=True)
            o = jnp.dot(e.astype(_BF16), vv, preferred_element_type=_F32) / l
            og_ref[g, rows_c, :] = o
            lse_ref[g, rows_c, :] = jnp.broadcast_to(m + jnp.log(l), (blk, LANES))
            return 0

        lax.fori_loop(0, dil * nb, body, 0, unroll=16)

    chunk = 256
    for c in range(seq_len // chunk):
        sl = pl.ds(c * chunk, chunk)
        lses = [lse_ref[g, sl, :] for g in range(ng)]
        mx = functools.reduce(jnp.maximum, lses)
        ws = [jnp.exp(x - mx) for x in lses]
        num = functools.reduce(lambda a, b: a + b, [w * og_ref[g, sl, :] for g, w in enumerate(ws)])
        den = functools.reduce(lambda a, b: a + b, ws)
        o_ref[sl, :] = (num / den).astype(o_ref.dtype)


def dilated_attn_prompt(q, kv, bsz, seq_len, groups, heads_per_group):
    ng = len(groups)
    nh = ng * heads_per_group
    slopes = _alibi_slopes(nh)
    blk_spec = lambda off: pl.BlockSpec((seq_len, LANES), lambda b, j, off=off: (b, off + j))
    in_specs = [pl.BlockSpec(memory_space=pltpu.SMEM)]
    in_specs += [blk_spec(g * heads_per_group) for g in range(ng)]
    in_specs += [blk_spec(g * heads_per_group) for g in range(ng)]
    in_specs += [blk_spec(nh + g * heads_per_group) for g in range(ng)]
    return pl.pallas_call(
        functools.partial(_dilated_prompt_kernel, groups=groups, seq_len=seq_len,
                          heads_per_group=heads_per_group, scale=LANES ** -0.5),
        grid=(bsz, heads_per_group),
        in_specs=in_specs,
        out_specs=pl.BlockSpec((seq_len, LANES), lambda b, j: (b, j)),
        out_shape=jax.ShapeDtypeStruct((bsz * seq_len, heads_per_group * LANES), _BF16),
        scratch_shapes=[pltpu.VMEM((ng, seq_len, LANES), _F32), pltpu.VMEM((ng, seq_len, LANES), _F32)],
        compiler_params=_params("arbitrary", "arbitrary"),
        name="dilated_attn_prompt",
    )(slopes, *([q] * ng), *([kv] * ng), *([kv] * ng))


def _dilated_sample_kernel(q_ref, new_ref, *refs, groups, steps, scale):
    ng = len(groups)
    st_refs, bp_refs, bn_refs = refs[:ng], refs[ng:2 * ng], refs[2 * ng:3 * ng]
    o_ref = refs[3 * ng]
    for t in range(steps):
        ms, ls, accs = [], [], []
        for g, (_, dil) in enumerate(groups):
            sub = t if dil > 1 else 0
            qb = q_ref[t, g].astype(_BF16)
            flat = lambda x: x.reshape(x.shape[0] * x.shape[1], LANES).astype(_BF16)
            kp, vp = flat(st_refs[g][:, sub, 0]), flat(st_refs[g][:, sub, 1])
            kn, vn = flat(new_ref[:, 0, g]), flat(new_ref[:, 1, g])
            sp = lax.dot_general(qb, kp, _NT, preferred_element_type=_F32) * scale + bp_refs[g][t]
            sn = lax.dot_general(qb, kn, _NT, preferred_element_type=_F32) * scale + bn_refs[g][t]
            m = jnp.maximum(jnp.max(sp, axis=-1, keepdims=True), jnp.max(sn, axis=-1, keepdims=True))
            ep = jnp.exp(sp - m)
            en = jnp.exp(sn - m)
            ls.append(jnp.sum(ep, axis=-1, keepdims=True) + jnp.sum(en, axis=-1, keepdims=True))
            accs.append(jnp.dot(ep.astype(_BF16), vp, preferred_element_type=_F32)
                        + jnp.dot(en.astype(_BF16), vn, preferred_element_type=_F32))
            ms.append(m)
        mx = functools.reduce(jnp.maximum, ms)
        ws = [jnp.exp(m - mx) for m in ms]
        num = functools.reduce(lambda a, b: a + b, [w * o for w, o in zip(ws, accs)])
        den = functools.reduce(lambda a, b: a + b, [w * l for w, l in zip(ws, ls)])
        o_ref[t] = num / den


def dilated_attn_sample(q, kv_new, states, layer, groups, heads_per_group, db, steps):
    ng = len(groups)
    hg = heads_per_group
    new_keys = 16
    assert steps <= new_keys and all(steps <= dil or dil == 1 for _, dil in groups)
    q5 = q.reshape(db, steps, ng, hg, LANES)
    new = jnp.pad(kv_new.reshape(db, steps, 2, ng, hg, LANES),
                  ((0, 0), (0, new_keys - steps)) + ((0, 0),) * 4)
    slopes = _alibi_slopes(ng * hg).reshape(ng, hg, 1)
    head = jnp.arange(hg)[:, None]
    t = jnp.arange(steps)[:, None, None]
    views, st_specs, bias_p, bias_n = [], [], [], []
    for g, (window, dil) in enumerate(groups):
        st = states[g]
        assert st.shape[2] == window
        nb = window // dil
        views.append(st.reshape(st.shape[0], db, nb, dil, 2, hg, LANES))
        st_specs.append(pl.BlockSpec((None, None, nb, min(dil, steps), 2, hg, LANES),
                                     lambda b: (layer, b, 0, 0, 0, 0, 0)))
        col = jnp.arange(nb * hg)[None, :]
        m, hp = col // hg, col % hg
        coln = jnp.arange(new_keys * hg)[None, :]
        tn, hn = coln // hg, coln % hg
        if dil == 1:
            back_p, ok_p = nb + t - m, (hp == head) & (m >= t)
            back_n, ok_n = t - tn, (hn == head) & (tn <= t)
        else:
            back_p, ok_p = nb - m + 0 * t, (hp == head) & (t >= 0)
            back_n, ok_n = 0 * (t - tn), (hn == head) & (tn == t)
        bias_p.append(jnp.where(ok_p, -slopes[g] * (back_p * dil).astype(_F32), NEG_INF))
        bias_n.append(jnp.where(ok_n, -slopes[g] * (back_n * dil).astype(_F32), NEG_INF))
    full = lambda x: pl.BlockSpec(x.shape, lambda b: (0,) * x.ndim)
    o = pl.pallas_call(
        functools.partial(_dilated_sample_kernel, groups=groups, steps=steps, scale=LANES ** -0.5),
        grid=(db,),
        in_specs=[pl.BlockSpec((None, steps, ng, hg, LANES), lambda b: (b, 0, 0, 0, 0)),
                  pl.BlockSpec((None, new_keys, 2, ng, hg, LANES), lambda b: (b, 0, 0, 0, 0, 0))]
                 + st_specs + [full(x) for x in bias_p] + [full(x) for x in bias_n],
        out_specs=pl.BlockSpec((None, steps, hg, LANES), lambda b: (b, 0, 0, 0)),
        out_shape=jax.ShapeDtypeStruct((db, steps, hg, LANES), _F32),
        compiler_params=_params("arbitrary"),
        name="dilated_attn_sample",
    )(q5, new, *views, *bias_p, *bias_n)
    return o.reshape(db * steps, hg * LANES).astype(_BF16)


def _shift_state_kernel(cur_ref, nxt_ref, new_ref, out_ref, *, steps):
    rows = cur_ref.shape[0]
    out_ref[0:rows - steps] = cur_ref[steps:rows]
    last = pl.program_id(2) == pl.num_programs(2) - 1

    @pl.when(last)
    def _():
        out_ref[rows - steps:rows] = new_ref[...]

    @pl.when(jnp.logical_not(last))
    def _():
        out_ref[rows - steps:rows] = nxt_ref[...]


def shift_window_state(state, new):
    layers, db, window = state.shape[:3]
    steps = new.shape[2]
    rows = min(window, 512)
    assert window % rows == 0 and rows % steps == 0 and rows > steps
    chunks = window // rows
    per = rows // steps
    tile = state.shape[3:]
    nil = (0,) * len(tile)
    return pl.pallas_call(
        functools.partial(_shift_state_kernel, steps=steps),
        grid=(layers, db, chunks),
        in_specs=[pl.BlockSpec((None, None, rows) + tile, lambda l, b, c: (l, b, c) + nil),
                  pl.BlockSpec((None, None, steps) + tile,
                               lambda l, b, c: (l, b, jnp.minimum(c + 1, chunks - 1) * per) + nil),
                  pl.BlockSpec((None, None, steps) + tile, lambda l, b, c: (l, b, 0) + nil)],
        out_specs=pl.BlockSpec((None, None, rows) + tile, lambda l, b, c: (l, b, c) + nil),
        out_shape=jax.ShapeDtypeStruct(state.shape, state.dtype),
        compiler_params=_params("arbitrary", "arbitrary", "arbitrary"),
        name="shift_window_state",
    )(state, state, new)


def kernel(x_prompt, x_sample, cache_kv_diff, state_kv_w128, state_kv_w512, state_kv_w2048,
           state_conv, page_table, attn_norm, ffn_norm, a_w_qkv, a_w_o, a_q_gain, a_k_gain,
           a_lambda, a_sub_gain, b_w_qkv, b_w_o, b_q_gain, b_k_gain, ffn_w_gate, ffn_w_up,
           ffn_w_down, ffn_conv_w, ffn_conv_b):
    bsz, seq_len, d_model = x_prompt.shape
    db, steps, _ = x_sample.shape
    depth = attn_norm.shape[0]
    d_ff = ffn_w_gate.shape[2]
    a_heads = a_w_qkv.shape[2] // (3 * LANES)
    ng = len(B_GROUPS)
    b_heads = b_w_qkv.shape[2] // (3 * LANES)
    hg = b_heads // ng
    aw = a_heads * LANES
    bw = b_heads * LANES
    mp, ms = bsz * seq_len, db * steps
    tm, tn = min(1024, seq_len), 512
    tm_down = min(512, seq_len)
    tq = min(512, seq_len)
    pages_per_step = 4
    win_states = (state_kv_w128, state_kv_w512, state_kv_w2048)
    cache = cache_kv_diff

    xp = x_prompt.reshape(mp, d_model)
    xs = x_sample.reshape(ms, d_model)
    n_a_layers = (depth + 1) // 2
    kv_diff_all, kv_diff_s = jnp.zeros((n_a_layers, mp, 2 * aw), _F32), []
    win_p = [[] for _ in B_GROUPS]
    win_s = [[] for _ in B_GROUPS]
    conv_p, conv_s = [], []

    for i in range(depth):
        hp = rmsnorm_bf16(xp, attn_norm[i], tm)
        hs = rmsnorm_bf16(xs, attn_norm[i], ms)
        if i % 2 == 0:
            a = i // 2
            lam_init = 0.8 - 0.6 * math.exp(-0.3 * i)
            w = a_w_qkv
            qp = matmul_headnorm(hp, w, a, a_q_gain[a], 0, aw, aw, LANES // 2, _BF16, tm, tn)
            kv_diff_all = matmul_headnorm(hp, w, a, a_k_gain[a], aw, 2 * aw, aw, LANES // 2, _F32,
                                          tm, tn, stack=(kv_diff_all, a))
            qs = matmul_headnorm(hs, w, a, a_q_gain[a], 0, aw, aw, LANES // 2, _BF16, ms, tn)
            kvs = matmul_headnorm(hs, w, a, a_k_gain[a], aw, 2 * aw, aw, LANES // 2, _F32, ms, tn)
            op = diff_attn_prompt(qp, kv_diff_all, a, a_lambda[a], a_sub_gain[a], lam_init, bsz,
                                  seq_len, a_heads, tq)
            os_ = diff_attn_sample(qs, kvs, cache, a, page_table, a_lambda[a], a_sub_gain[a],
                                   lam_init, a_heads, steps, pages_per_step)
            xp = matmul_residual(op, a_w_o, a, xp, tm, tn, after=os_)
            xs = matmul_residual(os_, a_w_o, a, xs, ms, tn)
            kv_diff_s.append(kvs.reshape(db, steps, 2, a_heads, LANES))
        else:
            b = i // 2
            w = b_w_qkv
            qp = matmul_headnorm(hp, w, b, b_q_gain[b], 0, bw, bw, LANES, _F32, tm, tn)
            kvp = matmul_headnorm(hp, w, b, b_k_gain[b], bw, 2 * bw, bw, LANES, _F32, tm, tn)
            qs = matmul_headnorm(hs, w, b, b_q_gain[b], 0, bw, bw, LANES, _F32, ms, tn)
            kvs = matmul_headnorm(hs, w, b, b_k_gain[b], bw, 2 * bw, bw, LANES, _F32, ms, tn)
            op = dilated_attn_prompt(qp, kvp, bsz, seq_len, B_GROUPS, hg)
            os_ = dilated_attn_sample(qs, kvs, win_states, b, B_GROUPS, hg, db, steps)
            xp = matmul_residual(op, b_w_o, b, xp, tm, tn)
            xs = matmul_residual(os_, b_w_o, b, xs, ms, tn)
            kp4 = kvp.reshape(bsz, seq_len, 2, ng, hg, LANES)
            ks4 = kvs.reshape(db, steps, 2, ng, hg, LANES)
            for g, (window, _) in enumerate(B_GROUPS):
                keep = min(window, seq_len)
                win_p[g].append(kp4[:, seq_len - keep:, :, g])
                win_s[g].append(ks4[:, :, :, g])
        hp = rmsnorm_bf16(xp, ffn_norm[i], tm)
        hs = rmsnorm_bf16(xs, ffn_norm[i], ms)
        actp, tail = ffn_gate_up_prompt(hp, ffn_w_gate, ffn_w_up, i, ffn_conv_w[i], ffn_conv_b[i],
                                        seq_len, tm, tn)
        acts, gs = ffn_gate_up_sample(hs, ffn_w_gate, ffn_w_up, i, ffn_conv_w[i], ffn_conv_b[i],
                                      state_conv[i], tn)
        xp = matmul_residual(actp, ffn_w_down, i, xp, tm_down, tn)
        xs = matmul_residual(acts, ffn_w_down, i, xs, ms, tn)
        conv_p.append(tail[:, 6:])
        conv_s.append(gs.reshape(db, steps, d_ff)[:, steps - 2:])

    new_state = [shift_window_state(win_states[g], jnp.stack(win_s[g])) for g in range(ng)]
    return (xp.reshape(bsz, seq_len, d_model), xs.reshape(db, steps, d_model),
            kv_diff_all.reshape(n_a_layers, bsz, seq_len, 2, a_heads, LANES), jnp.stack(kv_diff_s),
            jnp.stack(win_p[0]), new_state[0], jnp.stack(win_p[1]), new_state[1],
            jnp.stack(win_p[2]), new_state[2], jnp.stack(conv_p), jnp.stack(conv_s))
```
